```python
import jax, jax.numpy as jnp
from jax import lax
import numpy as np


D_MODEL = 2048
BATCH = 16
SEQ = 2048
DEPTH = 4

N_MIXERS = 2
N_LRU_LAYERS = (DEPTH + 1) // 2
N_ATT_LAYERS = DEPTH // 2
D_RNN = 2560
LRU_BLOCKS = 10
LRU_BLOCK_W = D_RNN // LRU_BLOCKS
CONV_W = 4
LRU_C = 8.0
N_HEADS = 16
HEAD_DIM = D_MODEL // N_HEADS
ROT_DIM = HEAD_DIM // 4
ROPE_THETA = 500000.0
DILATED_PAIRS = ((128, 1), (512, 4), (2048, 16))
SUB_BLOCK = 128
D_FF = 4 * D_MODEL
EPS = 1e-6
NEG_INF = -1e30

kernel_name = 'hybrid_rglru_dilated_attn_sqrelu'


def _rmsnorm(x, g):
    x32 = x.astype(jnp.float32)
    y = x32 * lax.rsqrt(jnp.mean(x32 * x32, axis=-1, keepdims=True) + EPS)
    return (y * g.astype(jnp.float32)).astype(x.dtype)


def _mlp(h, w1, w2):
    u = jnp.einsum('bsd,df->bsf', h, w1)
    u = jnp.square(jax.nn.relu(u))
    return jnp.einsum('bsf,fd->bsd', u, w2)


def _block_diag(x, w, b):
    bsz, s, _ = x.shape
    xr = x.reshape(bsz, s, LRU_BLOCKS, LRU_BLOCK_W)
    y = jnp.einsum('bshi,hij->bshj', xr, w) + b
    return y.reshape(bsz, s, D_RNN)


def _rglru_block(h, w_in, conv_w, conv_b, w_a, b_a, w_x, b_x, lam, w_out):
    xg = jnp.einsum('bsd,de->bse', h, w_in)
    xb, gb = xg[..., :D_RNN], xg[..., D_RNN:]
    xb = lax.conv_general_dilated(
        xb, conv_w[:, None, :], window_strides=(1,), padding=[(CONV_W - 1, 0)],
        dimension_numbers=('NWC', 'WIO', 'NWC'), feature_group_count=D_RNN) + conv_b
    r = jax.nn.sigmoid(_block_diag(xb, w_a, b_a).astype(jnp.float32))
    i = jax.nn.sigmoid(_block_diag(xb, w_x, b_x).astype(jnp.float32))
    log_a = -LRU_C * r * jax.nn.softplus(-lam.astype(jnp.float32))
    a = jnp.exp(log_a)
    mult = jnp.sqrt(-jnp.expm1(2.0 * log_a))
    u = xb.astype(jnp.float32) * i * mult

    def step(hprev, inp):
        a_t, u_t = inp
        hn = a_t * hprev + u_t
        return hn, hn

    bsz = h.shape[0]
    h0 = jnp.zeros((bsz, D_RNN), jnp.float32)
    _, hs = lax.scan(step, h0, (jnp.swapaxes(a, 0, 1), jnp.swapaxes(u, 0, 1)))
    y = jnp.swapaxes(hs, 0, 1)
    y = (y * jax.nn.gelu(gb.astype(jnp.float32))).astype(h.dtype)
    return jnp.einsum('bse,ed->bsd', y, w_out)


def _rope_partial(t):
    s = t.shape[1]
    half = ROT_DIM // 2
    inv = ROPE_THETA ** (-jnp.arange(0, ROT_DIM, 2, dtype=jnp.float32) / ROT_DIM)
    ang = jnp.arange(s, dtype=jnp.float32)[:, None] * inv[None, :]
    cos = jnp.cos(ang)[None, :, None, :]
    sin = jnp.sin(ang)[None, :, None, :]
    t32 = t.astype(jnp.float32)
    x1, x2 = t32[..., :half], t32[..., half:ROT_DIM]
    out = jnp.concatenate([x1 * cos - x2 * sin, x2 * cos + x1 * sin, t32[..., ROT_DIM:]], axis=-1)
    return out.astype(t.dtype)


def _dilated_branch(q, k, v, window, dilation):
    bsz, s, h, dh = q.shape
    d = dilation
    w_sub = window // d
    L = s // d
    nblk = -(-L // SUB_BLOCK)
    Lp = nblk * SUB_BLOCK

    def gather(t):
        t = t.reshape(bsz, L, d, h, dh).transpose(0, 2, 3, 1, 4)
        t = jnp.pad(t, ((0, 0), (0, 0), (0, 0), (0, Lp - L), (0, 0)))
        return t.reshape(bsz, d, h, nblk, SUB_BLOCK, dh)

    def band(t):
        prev = jnp.pad(t[:, :, :, :-1], ((0, 0), (0, 0), (0, 0), (1, 0), (0, 0), (0, 0)))
        return jnp.concatenate([prev, t], axis=-2)

    qb = gather(q)
    kw = band(gather(k))
    vw = band(gather(v))
    scores = jnp.einsum('brhnqd,brhnkd->brhnqk', qb, kw).astype(jnp.float32) * (HEAD_DIM ** -0.5)
    q_idx = jnp.arange(nblk)[:, None, None] * SUB_BLOCK + jnp.arange(SUB_BLOCK)[None, :, None]
    k_idx = (jnp.arange(nblk)[:, None, None] - 1) * SUB_BLOCK + jnp.arange(2 * SUB_BLOCK)[None, None, :]
    dist = q_idx - k_idx
    mask = (dist >= 0) & (dist <= w_sub) & (k_idx >= 0)
    scores = jnp.where(mask, scores, NEG_INF)
    m = jnp.max(scores, axis=-1, keepdims=True)
    p = jnp.exp(scores - m)
    denom = jnp.sum(p, axis=-1)
    o = jnp.einsum('brhnqk,brhnkd->brhnqd', p.astype(v.dtype), vw).astype(jnp.float32)
    o = o / denom[..., None]
    lse = m[..., 0] + jnp.log(denom)
    o = o.reshape(bsz, d, h, Lp, dh)[:, :, :, :L].transpose(0, 3, 1, 2, 4).reshape(bsz, s, h, dh)
    lse = lse.reshape(bsz, d, h, Lp)[..., :L].transpose(0, 3, 1, 2).reshape(bsz, s, h)
    return o, lse


def _dilated_attention_block(h, w_qkv, w_o):
    bsz, s, _ = h.shape
    qkv = jnp.einsum('bsd,de->bse', h, w_qkv).reshape(bsz, s, 3, N_HEADS, HEAD_DIM)
    q = _rope_partial(qkv[:, :, 0])
    k = _rope_partial(qkv[:, :, 1])
    v = qkv[:, :, 2]
    outs, lses = [], []
    for window, dilation in DILATED_PAIRS:
        o_g, lse_g = _dilated_branch(q, k, v, window, dilation)
        outs.append(o_g)
        lses.append(lse_g)
    outs = jnp.stack(outs, axis=0)
    wts = jax.nn.softmax(jnp.stack(lses, axis=0), axis=0)
    o = jnp.sum(wts[..., None] * outs, axis=0).astype(h.dtype)
    return jnp.einsum('bse,ed->bsd', o.reshape(bsz, s, D_MODEL), w_o)


def setup_inputs(seed: int = 0) -> dict:
    key = jax.random.key(seed)
    ks = jax.random.split(key, 20)
    f32 = jnp.float32
    nrm = lambda k, shape, scale: jax.random.normal(k, shape, f32) * scale
    x = jax.random.normal(ks[0], (BATCH, SEQ, D_MODEL), f32)
    mix_norm = 1.0 + nrm(ks[1], (DEPTH, D_MODEL), 0.02)
    mlp_norm = 1.0 + nrm(ks[2], (DEPTH, D_MODEL), 0.02)
    final_norm = 1.0 + nrm(ks[3], (D_MODEL,), 0.02)
    mlp_w1 = nrm(ks[4], (DEPTH, D_MODEL, D_FF), D_MODEL ** -0.5)
    mlp_w2 = nrm(ks[5], (DEPTH, D_FF, D_MODEL), D_FF ** -0.5)
    lru_w_in = nrm(ks[6], (N_LRU_LAYERS, D_MODEL, 2 * D_RNN), D_MODEL ** -0.5)
    lru_conv_w = nrm(ks[7], (N_LRU_LAYERS, CONV_W, D_RNN), CONV_W ** -0.5)
    lru_conv_b = nrm(ks[8], (N_LRU_LAYERS, D_RNN), 0.01)
    lru_w_a = nrm(ks[9], (N_LRU_LAYERS, LRU_BLOCKS, LRU_BLOCK_W, LRU_BLOCK_W), LRU_BLOCK_W ** -0.5)
    lru_b_a = nrm(ks[10], (N_LRU_LAYERS, LRU_BLOCKS, LRU_BLOCK_W), 0.01)
    lru_w_x = nrm(ks[11], (N_LRU_LAYERS, LRU_BLOCKS, LRU_BLOCK_W, LRU_BLOCK_W), LRU_BLOCK_W ** -0.5)
    lru_b_x = nrm(ks[12], (N_LRU_LAYERS, LRU_BLOCKS, LRU_BLOCK_W), 0.01)
    a_c = jax.random.uniform(ks[13], (N_LRU_LAYERS, D_RNN), f32, 0.9, 0.999)
    a0 = a_c ** (1.0 / LRU_C)
    lru_lambda = jnp.log(a0) - jnp.log1p(-a0)
    lru_w_out = nrm(ks[14], (N_LRU_LAYERS, D_RNN, D_MODEL), D_RNN ** -0.5)
    attn_w_qkv = nrm(ks[15], (N_ATT_LAYERS, D_MODEL, 3 * D_MODEL), D_MODEL ** -0.5)
    attn_w_o = nrm(ks[16], (N_ATT_LAYERS, D_MODEL, D_MODEL), D_MODEL ** -0.5)
    return {'x': x, 'mix_norm': mix_norm, 'mlp_norm': mlp_norm, 'final_norm': final_norm,
            'mlp_w1': mlp_w1, 'mlp_w2': mlp_w2,
            'lru_w_in': lru_w_in, 'lru_conv_w': lru_conv_w, 'lru_conv_b': lru_conv_b,
            'lru_w_a': lru_w_a, 'lru_b_a': lru_b_a, 'lru_w_x': lru_w_x, 'lru_b_x': lru_b_x,
            'lru_lambda': lru_lambda, 'lru_w_out': lru_w_out,
            'attn_w_qkv': attn_w_qkv, 'attn_w_o': attn_w_o}


def reference(x, mix_norm, mlp_norm, final_norm, mlp_w1, mlp_w2,
              lru_w_in, lru_conv_w, lru_conv_b, lru_w_a, lru_b_a, lru_w_x, lru_b_x,
              lru_lambda, lru_w_out, attn_w_qkv, attn_w_o):
    for i in range(DEPTH):
        h = _rmsnorm(x, mix_norm[i])
        j = i // N_MIXERS
        if i % N_MIXERS == 0:
            x = x + _rglru_block(h, lru_w_in[j], lru_conv_w[j], lru_conv_b[j],
                                 lru_w_a[j], lru_b_a[j], lru_w_x[j], lru_b_x[j],
                                 lru_lambda[j], lru_w_out[j])
        else:
            x = x + _dilated_attention_block(h, attn_w_qkv[j], attn_w_o[j])
        x = x + _mlp(_rmsnorm(x, mlp_norm[i]), mlp_w1[i], mlp_w2[i])
    return _rmsnorm(x, final_norm)
```

```python
import functools
import math

import jax
import jax.numpy as jnp
import numpy as np
from jax import lax
from jax.experimental import pallas as pl
from jax.experimental.pallas import tpu as pltpu

D_MODEL = 2048
BATCH = 16
SEQ = 2048
DEPTH = 4
N_MIXERS = 2
D_RNN = 2560
LRU_BLOCKS = 10
LRU_BLOCK_W = D_RNN // LRU_BLOCKS
CONV_W = 4
LRU_C = 8.0
N_HEADS = 16
HEAD_DIM = D_MODEL // N_HEADS
ROT_DIM = HEAD_DIM // 4
ROPE_THETA = 500000.0
DILATED_PAIRS = ((128, 1), (512, 4), (2048, 16))
SUB_BLOCK = 128
D_FF = 4 * D_MODEL
EPS = 1e-6
NEG_INF = -1e30

V7X_VMEM_BYTES = 64 * 1024 * 1024
V7X_VMEM_RESERVED_BYTES = 4 * 1024 * 1024
MOSAIC_TEMP_BYTES = 10 * 1024 * 1024
SUBLANES = 8
LANES = 128

N_CLASSES = 16
CLASS_LEN = SEQ // N_CLASSES

F32 = jnp.float32
BF16 = jnp.bfloat16


def _vmem_limit(*buffer_bytes):
    need = sum(buffer_bytes) + MOSAIC_TEMP_BYTES
    return int(min(need, V7X_VMEM_BYTES - V7X_VMEM_RESERVED_BYTES))


def _nbytes(shape, dtype):
    return int(np.prod(shape)) * jnp.dtype(dtype).itemsize


def _rms_rows(x, g):
    ms = jnp.mean(x * x, axis=-1, keepdims=True)
    return (x * lax.rsqrt(ms + EPS) * g).astype(BF16)


MLP_TM = 1024
MLP_TF = 512
NORM_CHUNK = 128


def _mlp_kernel(x_ref, g_ref, w1_ref, w2_ref, gf_ref, o_ref, h_ref, *, final_norm):
    f = pl.program_id(1)

    @pl.when(f == 0)
    def _():
        def body(c, carry):
            rows = pl.ds(pl.multiple_of(c * NORM_CHUNK, NORM_CHUNK), NORM_CHUNK)
            x = x_ref[rows, :]
            h_ref[rows, :] = _rms_rows(x, g_ref[...])
            o_ref[rows, :] = x
            return carry
        lax.fori_loop(0, MLP_TM // NORM_CHUNK, body, 0)

    u = jnp.dot(h_ref[...], w1_ref[...], preferred_element_type=F32)
    u = jnp.square(jnp.maximum(u, 0.0)).astype(BF16)
    o_ref[...] += jnp.dot(u, w2_ref[...], preferred_element_type=F32)

    if final_norm:
        @pl.when(f == pl.num_programs(1) - 1)
        def _():
            def body(c, carry):
                rows = pl.ds(pl.multiple_of(c * NORM_CHUNK, NORM_CHUNK), NORM_CHUNK)
                y = o_ref[rows, :]
                ms = jnp.mean(y * y, axis=-1, keepdims=True)
                o_ref[rows, :] = y * lax.rsqrt(ms + EPS) * gf_ref[...]
                return carry
            lax.fori_loop(0, MLP_TM // NORM_CHUNK, body, 0)


def _mlp(x2d, g, w1, w2, gf, final_norm):
    rows = x2d.shape[0]
    vmem = _vmem_limit(
        2 * _nbytes((MLP_TM, D_MODEL), F32), 2 * _nbytes((MLP_TM, D_MODEL), F32),
        _nbytes((MLP_TM, D_MODEL), BF16),
        2 * _nbytes((D_MODEL, MLP_TF), BF16), 2 * _nbytes((MLP_TF, D_MODEL), BF16))
    return pl.pallas_call(
        functools.partial(_mlp_kernel, final_norm=final_norm),
        grid=(rows // MLP_TM, D_FF // MLP_TF),
        in_specs=[
            pl.BlockSpec((MLP_TM, D_MODEL), lambda i, f: (i, 0)),
            pl.BlockSpec((1, D_MODEL), lambda i, f: (0, 0)),
            pl.BlockSpec((D_MODEL, MLP_TF), lambda i, f: (0, f)),
            pl.BlockSpec((MLP_TF, D_MODEL), lambda i, f: (f, 0)),
            pl.BlockSpec((1, D_MODEL), lambda i, f: (0, 0)),
        ],
        out_specs=pl.BlockSpec((MLP_TM, D_MODEL), lambda i, f: (i, 0)),
        out_shape=jax.ShapeDtypeStruct(x2d.shape, F32),
        scratch_shapes=[pltpu.VMEM((MLP_TM, D_MODEL), BF16)],
        compiler_params=pltpu.CompilerParams(
            dimension_semantics=("parallel", "arbitrary"), vmem_limit_bytes=vmem),
        name="mlp",
    )(x2d, g, w1, w2, gf)


LRU_TS = 64
LRU_TN = 1024
LRU_CORE_TS = 128
LRU_CORE_CT = LRU_BLOCK_W
LRU_OUT_TS = 32


def _lru_in_kernel(x_ref, g_ref, w_ref, o_ref, h_ref):
    @pl.when(pl.program_id(1) == 0)
    def _():
        for s in range(LRU_TS):
            x = x_ref[:, s * D_MODEL:(s + 1) * D_MODEL]
            h_ref[s * BATCH:(s + 1) * BATCH, :] = _rms_rows(x, g_ref[...])

    res = jnp.dot(h_ref[...], w_ref[...], preferred_element_type=F32)
    o_ref[...] = res.reshape(LRU_TS, BATCH, LRU_TN).astype(o_ref.dtype)


def _lru_in(x, g, w_in):
    xv = x.reshape(BATCH, SEQ * D_MODEL)
    n_out = 2 * D_RNN
    vmem = _vmem_limit(
        2 * _nbytes((BATCH, LRU_TS * D_MODEL), F32), 2 * _nbytes((D_MODEL, LRU_TN), BF16),
        2 * _nbytes((LRU_TS, BATCH, LRU_TN), BF16), _nbytes((LRU_TS * BATCH, D_MODEL), BF16))
    return pl.pallas_call(
        _lru_in_kernel,
        grid=(SEQ // LRU_TS, n_out // LRU_TN),
        in_specs=[
            pl.BlockSpec((BATCH, LRU_TS * D_MODEL), lambda i, j: (0, i)),
            pl.BlockSpec((1, D_MODEL), lambda i, j: (0, 0)),
            pl.BlockSpec((D_MODEL, LRU_TN), lambda i, j: (0, j)),
        ],
        out_specs=pl.BlockSpec((LRU_TS, BATCH, LRU_TN), lambda i, j: (i, 0, j)),
        out_shape=jax.ShapeDtypeStruct((SEQ, BATCH, n_out), BF16),
        scratch_shapes=[pltpu.VMEM((LRU_TS * BATCH, D_MODEL), BF16)],
        compiler_params=pltpu.CompilerParams(
            dimension_semantics=("parallel", "arbitrary"), vmem_limit_bytes=vmem),
        name="lru_in",
    )(xv, g, w_in)


def _lru_core_kernel(xb_ref, gb_ref, cw_ref, cb_ref, wa_ref, ba_ref, wx_ref, bx_ref, lam_ref,
                     y_ref, ext_ref, a_ref, u_ref, h_ref):
    ts, ct = LRU_CORE_TS, LRU_CORE_CT
    pad = CONV_W - 1

    @pl.when(pl.program_id(1) == 0)
    def _():
        ext_ref[0:pad] = jnp.zeros((pad, BATCH, ct), F32)
        h_ref[...] = jnp.zeros((BATCH, ct), F32)

    ext_ref[pad:pad + ts] = xb_ref[...].astype(F32)
    xc = cb_ref[...].reshape(1, 1, ct) + sum(
        cw_ref[k:k + 1, :].reshape(1, 1, ct) * ext_ref[k:k + ts] for k in range(CONV_W))
    ext_ref[0:pad] = ext_ref[ts:ts + pad]

    xc2 = xc.reshape(ts * BATCH, ct)
    xcb = xc2.astype(BF16)
    r = jax.nn.sigmoid(jnp.dot(xcb, wa_ref[...], preferred_element_type=F32) + ba_ref[...])
    i = jax.nn.sigmoid(jnp.dot(xcb, wx_ref[...], preferred_element_type=F32) + bx_ref[...])
    z = -lam_ref[...]
    softplus = jnp.maximum(z, 0.0) + jnp.log1p(jnp.exp(-jnp.abs(z)))
    log_a = (-LRU_C) * r * softplus
    a_ref[...] = jnp.exp(log_a).reshape(ts, BATCH, ct)
    th = jnp.tanh(log_a)
    u_ref[...] = (xc2 * i * jnp.sqrt(-2.0 * th / (1.0 - th))).reshape(ts, BATCH, ct)

    def step(s, h):
        h = a_ref[s] * h + u_ref[s]
        u_ref[s] = h
        return h
    h_ref[...] = lax.fori_loop(0, ts, step, h_ref[...], unroll=8)

    y_ref[...] = (u_ref[...] * jax.nn.gelu(gb_ref[...].astype(F32))).astype(y_ref.dtype)


def _lru_core(xg, conv_w, conv_b, w_a, b_a, w_x, b_x, lam):
    ts, ct = LRU_CORE_TS, LRU_CORE_CT
    n_cb = D_RNN // ct
    slab = _nbytes((ts, BATCH, ct), F32)
    vmem = _vmem_limit(4 * _nbytes((ts, BATCH, ct), BF16), 2 * _nbytes((ts, BATCH, ct), BF16),
                       4 * _nbytes((ct, ct), BF16), 3 * slab, 8 * slab)
    vec = lambda: pl.BlockSpec((1, ct), lambda c, t: (0, c))
    return pl.pallas_call(
        _lru_core_kernel,
        grid=(n_cb, SEQ // ts),
        in_specs=[
            pl.BlockSpec((ts, BATCH, ct), lambda c, t: (t, 0, c)),
            pl.BlockSpec((ts, BATCH, ct), lambda c, t: (t, 0, c + n_cb)),
            pl.BlockSpec((CONV_W, ct), lambda c, t: (0, c)),
            vec(),
            pl.BlockSpec((None, ct, ct), lambda c, t: (c, 0, 0)),
            vec(),
            pl.BlockSpec((None, ct, ct), lambda c, t: (c, 0, 0)),
            vec(),
            vec(),
        ],
        out_specs=pl.BlockSpec((ts, BATCH, ct), lambda c, t: (t, 0, c)),
        out_shape=jax.ShapeDtypeStruct((SEQ, BATCH, D_RNN), BF16),
        scratch_shapes=[
            pltpu.VMEM((ts + CONV_W - 1, BATCH, ct), F32),
            pltpu.VMEM((ts, BATCH, ct), F32),
            pltpu.VMEM((ts, BATCH, ct), F32),
            pltpu.VMEM((BATCH, ct), F32),
        ],
        compiler_params=pltpu.CompilerParams(
            dimension_semantics=("parallel", "arbitrary"), vmem_limit_bytes=vmem),
        name="lru_core",
    )(xg, xg, conv_w, conv_b, w_a, b_a, w_x, b_x, lam)


def _lru_out_kernel(y_ref, w_ref, x_ref, o_ref):
    ts = LRU_OUT_TS
    res = jnp.dot(y_ref[...].reshape(ts * BATCH, D_RNN), w_ref[...], preferred_element_type=F32)
    for s in range(ts):
        cols = slice(s * D_MODEL, (s + 1) * D_MODEL)
        o_ref[:, cols] = x_ref[:, cols] + res[s * BATCH:(s + 1) * BATCH, :]


def _lru_out(y, w_out, x):
    ts = LRU_OUT_TS
    xv = x.reshape(BATCH, SEQ * D_MODEL)
    vmem = _vmem_limit(
        2 * _nbytes((ts, BATCH, D_RNN), BF16), 2 * _nbytes((D_RNN, D_MODEL), BF16),
        4 * _nbytes((BATCH, ts * D_MODEL), F32))
    out = pl.pallas_call(
        _lru_out_kernel,
        grid=(SEQ // ts,),
        in_specs=[
            pl.BlockSpec((ts, BATCH, D_RNN), lambda i: (i, 0, 0)),
            pl.BlockSpec((D_RNN, D_MODEL), lambda i: (0, 0)),
            pl.BlockSpec((BATCH, ts * D_MODEL), lambda i: (0, i)),
        ],
        out_specs=pl.BlockSpec((BATCH, ts * D_MODEL), lambda i: (0, i)),
        out_shape=jax.ShapeDtypeStruct(xv.shape, F32),
        compiler_params=pltpu.CompilerParams(
            dimension_semantics=("parallel",), vmem_limit_bytes=vmem),
        name="lru_out",
    )(y, w_out, xv)
    return out.reshape(BATCH, SEQ, D_MODEL)


QKV_CLASSES = 8
QKV_TN = 1024
ATT_HEADS = 2
AO_CLASSES = 4
AO_TN = 1024


def _rope_tables():
    half = ROT_DIM // 2
    inv = ROPE_THETA ** (-jnp.arange(0, ROT_DIM, 2, dtype=F32) / ROT_DIM)
    pos = (jnp.arange(CLASS_LEN, dtype=F32)[None, :] * N_CLASSES
           + jnp.arange(N_CLASSES, dtype=F32)[:, None])
    ang = pos[..., None] * inv
    cos, sin = jnp.cos(ang), jnp.sin(ang)
    ones = jnp.ones(pos.shape + (HEAD_DIM - ROT_DIM,), F32)
    zeros_h = jnp.zeros_like(cos)
    zeros_t = jnp.zeros_like(ones)
    c = jnp.concatenate([cos, cos, ones], axis=-1)
    s_up = jnp.concatenate([-sin, zeros_h, zeros_t], axis=-1)
    s_dn = jnp.concatenate([zeros_h, sin, zeros_t], axis=-1)
    return c, s_up, s_dn


def _qkv_kernel(x_ref, g_ref, w_ref, c_ref, su_ref, sd_ref, o_ref, h_ref):
    n = pl.program_id(2)
    rows = QKV_CLASSES * CLASS_LEN

    @pl.when(n == 0)
    def _():
        for k in range(QKV_CLASSES):
            x = x_ref[:, k * D_MODEL:(k + 1) * D_MODEL]
            h_ref[k * CLASS_LEN:(k + 1) * CLASS_LEN, :] = _rms_rows(x, g_ref[...])

    res = jnp.dot(h_ref[...], w_ref[...], preferred_element_type=F32)

    @pl.when(n < 2 * D_MODEL // QKV_TN)
    def _():
        c = c_ref[...].reshape(rows, HEAD_DIM)
        su = su_ref[...].reshape(rows, HEAD_DIM)
        sd = sd_ref[...].reshape(rows, HEAD_DIM)
        for hh in range(QKV_TN // HEAD_DIM):
            t = res[:, hh * HEAD_DIM:(hh + 1) * HEAD_DIM]
            up = pltpu.roll(t, HEAD_DIM - ROT_DIM // 2, axis=1)
            dn = pltpu.roll(t, ROT_DIM // 2, axis=1)
            o_ref[:, :, hh * HEAD_DIM:(hh + 1) * HEAD_DIM] = (
                t * c + up * su + dn * sd).reshape(QKV_CLASSES, CLASS_LEN, HEAD_DIM)

    @pl.when(n >= 2 * D_MODEL // QKV_TN)
    def _():
        o_ref[...] = res.reshape(QKV_CLASSES, CLASS_LEN, QKV_TN)


def _qkv(x, g, w_qkv, tables):
    xv = x.reshape(BATCH, CLASS_LEN, N_CLASSES * D_MODEL)
    n_out = 3 * D_MODEL
    rows = QKV_CLASSES * CLASS_LEN
    tab = pl.BlockSpec((QKV_CLASSES, CLASS_LEN, HEAD_DIM), lambda b, rb, n: (rb, 0, 0))
    vmem = _vmem_limit(
        2 * _nbytes((CLASS_LEN, QKV_CLASSES * D_MODEL), F32), 2 * _nbytes((D_MODEL, QKV_TN), BF16),
        2 * _nbytes((rows, QKV_TN), F32), _nbytes((rows, D_MODEL), BF16),
        6 * _nbytes((rows, HEAD_DIM), F32))
    return pl.pallas_call(
        _qkv_kernel,
        grid=(BATCH, N_CLASSES // QKV_CLASSES, n_out // QKV_TN),
        in_specs=[
            pl.BlockSpec((None, CLASS_LEN, QKV_CLASSES * D_MODEL), lambda b, rb, n: (b, 0, rb)),
            pl.BlockSpec((1, D_MODEL), lambda b, rb, n: (0, 0)),
            pl.BlockSpec((D_MODEL, QKV_TN), lambda b, rb, n: (0, n)),
            tab, tab, tab,
        ],
        out_specs=pl.BlockSpec((None, QKV_CLASSES, CLASS_LEN, QKV_TN), lambda b, rb, n: (b, rb, 0, n)),
        out_shape=jax.ShapeDtypeStruct((BATCH, N_CLASSES, CLASS_LEN, n_out), F32),
        scratch_shapes=[pltpu.VMEM((rows, D_MODEL), BF16)],
        compiler_params=pltpu.CompilerParams(
            dimension_semantics=("parallel", "parallel", "arbitrary"), vmem_limit_bytes=vmem),
        name="attn_qkv",
    )(xv, g, w_qkv, *tables)


def _tile_order(kind):
    a = lax.broadcasted_iota(jnp.int32, (SUB_BLOCK, SUB_BLOCK), 0)
    b = lax.broadcasted_iota(jnp.int32, (SUB_BLOCK, SUB_BLOCK), 1)
    if kind == 1:
        f = lambda v: N_CLASSES * (v % SUBLANES) + v // SUBLANES
    elif kind == 4:
        f = lambda v: 4 * (v % 32) + v // 32
    else:
        f = lambda v: v
    return f(a), f(b)


def _softmax_tile(q, ks, vs, masks):
    scale = HEAD_DIM ** -0.5
    ss = []
    for k, mk in zip(ks, masks):
        s = lax.dot_general(q, k, (((1,), (1,)), ((), ())), preferred_element_type=F32) * scale
        ss.append(jnp.where(mk, s, NEG_INF))
    m = functools.reduce(jnp.maximum, [s.max(axis=-1, keepdims=True) for s in ss])
    ps = [jnp.exp(s - m) for s in ss]
    l = functools.reduce(jnp.add, [p.sum(axis=-1, keepdims=True) for p in ps])
    acc = functools.reduce(jnp.add, [
        jnp.dot(p.astype(BF16), v, preferred_element_type=F32) for p, v in zip(ps, vs)])
    return m, l, acc


def _merge(m0, l0, acc0, m1, l1, acc1):
    m = jnp.maximum(m0, m1)
    a0 = jnp.exp(m0 - m)
    a1 = jnp.exp(m1 - m)
    return m, a0 * l0 + a1 * l1, a0 * acc0 + a1 * acc1


def _attn_kernel(q_ref, k_ref, v_ref, o_ref, acc_ref, m_ref, l_ref):
    ua1, ub1 = _tile_order(1)
    ua4, ub4 = _tile_order(4)
    ua16, ub16 = _tile_order(16)
    cur1, cur4, cur16 = ub1 <= ua1, ub4 <= ua4, ub16 <= ua16

    for hh in range(ATT_HEADS):
        cs = slice(hh * HEAD_DIM, (hh + 1) * HEAD_DIM)

        def d16(r, carry):
            q = q_ref[r, :, cs].astype(BF16)
            k = k_ref[r, :, cs].astype(BF16)
            v = v_ref[r, :, cs].astype(BF16)
            m, l, acc = _softmax_tile(q, [k], [v], [cur16])
            m_ref[r] = m
            l_ref[r] = l
            acc_ref[r] = acc
            return carry
        lax.fori_loop(0, N_CLASSES, d16, 0)

        def gather4(ref, r4, start, cols):
            return jnp.concatenate(
                [ref[r4 + 4 * c, pl.ds(start, 32), cols] for c in range(4)], axis=0)

        def scatter4(ref, r4, start, val):
            for c in range(4):
                ref[r4 + 4 * c, pl.ds(start, 32), :] = val[32 * c:32 * (c + 1)]

        for r4 in range(4):
            def d4(n4, carry, r4=r4):
                start = pl.multiple_of(n4 * 32, 32)
                prev = pl.multiple_of(jnp.maximum(n4 - 1, 0) * 32, 32)
                q = gather4(q_ref, r4, start, cs).astype(BF16)
                kc = gather4(k_ref, r4, start, cs).astype(BF16)
                vc = gather4(v_ref, r4, start, cs).astype(BF16)
                kp = gather4(k_ref, r4, prev, cs).astype(BF16)
                vp = gather4(v_ref, r4, prev, cs).astype(BF16)
                no_prev = jnp.where(n4 > 0, 0, SUB_BLOCK)
                m1, l1, acc1 = _softmax_tile(q, [kp, kc], [vp, vc], [ub4 >= ua4 + no_prev, cur4])
                m0 = gather4(m_ref, r4, start, slice(None))
                l0 = gather4(l_ref, r4, start, slice(None))
                acc0 = gather4(acc_ref, r4, start, slice(None))
                m, l, acc = _merge(m0, l0, acc0, m1, l1, acc1)
                scatter4(m_ref, r4, start, m)
                scatter4(l_ref, r4, start, l)
                scatter4(acc_ref, r4, start, acc)
                return carry
            lax.fori_loop(0, CLASS_LEN // 32, d4, 0)

        def gather1(ref, start, cols):
            t = ref[:, pl.ds(start, SUBLANES), cols]
            return t.reshape(N_CLASSES * SUBLANES, t.shape[-1])

        def d1(n, carry):
            start = pl.multiple_of(n * SUBLANES, SUBLANES)
            prev = pl.multiple_of(jnp.maximum(n - 1, 0) * SUBLANES, SUBLANES)
            q = gather1(q_ref, start, cs).astype(BF16)
            kc = gather1(k_ref, start, cs).astype(BF16)
            vc = gather1(v_ref, start, cs).astype(BF16)
            kp = gather1(k_ref, prev, cs).astype(BF16)
            vp = gather1(v_ref, prev, cs).astype(BF16)
            no_prev = jnp.where(n > 0, 0, SUB_BLOCK)
            m1, l1, acc1 = _softmax_tile(q, [kp, kc], [vp, vc], [ub1 >= ua1 + no_prev, cur1])
            m0 = gather1(m_ref, start, slice(None))
            l0 = gather1(l_ref, start, slice(None))
            acc0 = gather1(acc_ref, start, slice(None))
            _, l, acc = _merge(m0, l0, acc0, m1, l1, acc1)
            o = acc / l
            o_ref[:, pl.ds(start, SUBLANES), cs] = o.reshape(N_CLASSES, SUBLANES, HEAD_DIM).astype(o_ref.dtype)
            return carry
        lax.fori_loop(0, CLASS_LEN // SUBLANES, d1, 0)


def _attn(qkv):
    blk = (None, N_CLASSES, CLASS_LEN, ATT_HEADS * HEAD_DIM)
    n_hg = N_HEADS // ATT_HEADS
    tile = _nbytes((N_CLASSES, CLASS_LEN, ATT_HEADS * HEAD_DIM), F32)
    state = _nbytes((N_CLASSES, CLASS_LEN, HEAD_DIM), F32)
    vmem = _vmem_limit(6 * tile, 2 * tile, 3 * state)
    return pl.pallas_call(
        _attn_kernel,
        grid=(BATCH, n_hg),
        in_specs=[
            pl.BlockSpec(blk, lambda b, g: (b, 0, 0, g)),
            pl.BlockSpec(blk, lambda b, g: (b, 0, 0, n_hg + g)),
            pl.BlockSpec(blk, lambda b, g: (b, 0, 0, 2 * n_hg + g)),
        ],
        out_specs=pl.BlockSpec(blk, lambda b, g: (b, 0, 0, g)),
        out_shape=jax.ShapeDtypeStruct((BATCH, N_CLASSES, CLASS_LEN, D_MODEL), F32),
        scratch_shapes=[
            pltpu.VMEM((N_CLASSES, CLASS_LEN, HEAD_DIM), F32),
            pltpu.VMEM((N_CLASSES, CLASS_LEN, 1), F32),
            pltpu.VMEM((N_CLASSES, CLASS_LEN, 1), F32),
        ],
        compiler_params=pltpu.CompilerParams(
            dimension_semantics=("parallel", "parallel"), vmem_limit_bytes=vmem),
        name="attn_core",
    )(qkv, qkv, qkv)


def _attn_out_kernel(a_ref, w_ref, x_ref, o_ref):
    rows = AO_CLASSES * CLASS_LEN
    a = a_ref[...].reshape(rows, D_MODEL).astype(BF16)
    for nh in range(D_MODEL // AO_TN):
        res = jnp.dot(a, w_ref[:, nh * AO_TN:(nh + 1) * AO_TN], preferred_element_type=F32)
        for k in range(AO_CLASSES):
            cols = slice(k * D_MODEL + nh * AO_TN, k * D_MODEL + (nh + 1) * AO_TN)
            o_ref[:, cols] = x_ref[:, cols] + res[k * CLASS_LEN:(k + 1) * CLASS_LEN, :]


def _attn_out(att, w_o, x):
    xv = x.reshape(BATCH, CLASS_LEN, N_CLASSES * D_MODEL)
    xblk = pl.BlockSpec((None, CLASS_LEN, AO_CLASSES * D_MODEL), lambda b, rb: (b, 0, rb))
    vmem = _vmem_limit(
        2 * _nbytes((AO_CLASSES, CLASS_LEN, D_MODEL), F32), 2 * _nbytes((D_MODEL, D_MODEL), BF16),
        4 * _nbytes((CLASS_LEN, AO_CLASSES * D_MODEL), F32))
    out = pl.pallas_call(
        _attn_out_kernel,
        grid=(BATCH, N_CLASSES // AO_CLASSES),
        in_specs=[
            pl.BlockSpec((None, AO_CLASSES, CLASS_LEN, D_MODEL), lambda b, rb: (b, rb, 0, 0)),
            pl.BlockSpec((D_MODEL, D_MODEL), lambda b, rb: (0, 0)),
            xblk,
        ],
        out_specs=xblk,
        out_shape=jax.ShapeDtypeStruct(xv.shape, F32),
        compiler_params=pltpu.CompilerParams(
            dimension_semantics=("parallel", "parallel"), vmem_limit_bytes=vmem),
        name="attn_out",
    )(att, w_o, xv)
    return out.reshape(BATCH, SEQ, D_MODEL)


def kernel(x, mix_norm, mlp_norm, final_norm, mlp_w1, mlp_w2,
           lru_w_in, lru_conv_w, lru_conv_b, lru_w_a, lru_b_a, lru_w_x, lru_b_x,
           lru_lambda, lru_w_out, attn_w_qkv, attn_w_o):
    assert x.shape == (BATCH, SEQ, D_MODEL) and x.dtype == F32
    row = lambda v: v.reshape(1, -1)
    tables = _rope_tables()
    gf = row(final_norm)
    for i in range(DEPTH):
        j = i // N_MIXERS
        g = row(mix_norm[i])
        if i % N_MIXERS == 0:
            xg = _lru_in(x, g, lru_w_in[j].astype(BF16))
            y = _lru_core(xg, lru_conv_w[j], row(lru_conv_b[j]),
                          lru_w_a[j].astype(BF16), row(lru_b_a[j]),
                          lru_w_x[j].astype(BF16), row(lru_b_x[j]), row(lru_lambda[j]))
            x = _lru_out(y, lru_w_out[j].astype(BF16), x)
        else:
            qkv = _qkv(x, g, attn_w_qkv[j].astype(BF16), tables)
            att = _attn(qkv)
            x = _attn_out(att, attn_w_o[j].astype(BF16), x)
        x2d = _mlp(x.reshape(BATCH * SEQ, D_MODEL), row(mlp_norm[i]),
                   mlp_w1[i].astype(BF16), mlp_w2[i].astype(BF16), gf, i == DEPTH - 1)
        x = x2d.reshape(BATCH, SEQ, D_MODEL)
    return x
```

```python
import functools
import math

import jax
import jax.numpy as jnp
import numpy as np
from jax import lax
from jax.experimental import pallas as pl
from jax.experimental.pallas import tpu as pltpu

D_MODEL = 2048
BATCH = 16
SEQ = 2048
DEPTH = 4
N_MIXERS = 2
D_RNN = 2560
LRU_BLOCKS = 10
LRU_BLOCK_W = D_RNN // LRU_BLOCKS
CONV_W = 4
LRU_C = 8.0
N_HEADS = 16
HEAD_DIM = D_MODEL // N_HEADS
ROT_DIM = HEAD_DIM // 4
ROPE_THETA = 500000.0
DILATED_PAIRS = ((128, 1), (512, 4), (2048, 16))
SUB_BLOCK = 128
D_FF = 4 * D_MODEL
EPS = 1e-6
NEG_INF = -1e30

V7X_VMEM_BYTES = 64 * 1024 * 1024
V7X_VMEM_RESERVED_BYTES = 4 * 1024 * 1024
MOSAIC_TEMP_BYTES = 10 * 1024 * 1024
SUBLANES = 8
LANES = 128

N_CLASSES = 16
CLASS_LEN = SEQ // N_CLASSES

F32 = jnp.float32
BF16 = jnp.bfloat16


def _vmem_limit(*buffer_bytes):
    need = sum(buffer_bytes) + MOSAIC_TEMP_BYTES
    return int(min(need, V7X_VMEM_BYTES - V7X_VMEM_RESERVED_BYTES))


def _nbytes(shape, dtype):
    return int(np.prod(shape)) * jnp.dtype(dtype).itemsize


def _rms_rows(x, g):
    ms = jnp.mean(x * x, axis=-1, keepdims=True)
    return (x * lax.rsqrt(ms + EPS) * g).astype(BF16)


MLP_TM = 1024
MLP_TF = 512
NORM_CHUNK = 128


def _mlp_kernel(x_ref, g_ref, w1_ref, w2_ref, gf_ref, o_ref, h_ref, *, final_norm):
    f = pl.program_id(1)

    @pl.when(f == 0)
    def _():
        def body(c, carry):
            rows = pl.ds(pl.multiple_of(c * NORM_CHUNK, NORM_CHUNK), NORM_CHUNK)
            x = x_ref[rows, :]
            h_ref[rows, :] = _rms_rows(x, g_ref[...])
            o_ref[rows, :] = x
            return carry
        lax.fori_loop(0, MLP_TM // NORM_CHUNK, body, 0)

    u = jnp.dot(h_ref[...], w1_ref[...], preferred_element_type=F32)
    u = jnp.square(jnp.maximum(u, 0.0)).astype(BF16)
    o_ref[...] += jnp.dot(u, w2_ref[...], preferred_element_type=F32)

    if final_norm:
        @pl.when(f == pl.num_programs(1) - 1)
        def _():
            def body(c, carry):
                rows = pl.ds(pl.multiple_of(c * NORM_CHUNK, NORM_CHUNK), NORM_CHUNK)
                y = o_ref[rows, :]
                ms = jnp.mean(y * y, axis=-1, keepdims=True)
                o_ref[rows, :] = y * lax.rsqrt(ms + EPS) * gf_ref[...]
                return carry
            lax.fori_loop(0, MLP_TM // NORM_CHUNK, body, 0)


def _mlp(x2d, g, w1, w2, gf, final_norm):
    rows = x2d.shape[0]
    vmem = _vmem_limit(
        2 * _nbytes((MLP_TM, D_MODEL), F32), 2 * _nbytes((MLP_TM, D_MODEL), F32),
        _nbytes((MLP_TM, D_MODEL), BF16),
        2 * _nbytes((D_MODEL, MLP_TF), BF16), 2 * _nbytes((MLP_TF, D_MODEL), BF16))
    return pl.pallas_call(
        functools.partial(_mlp_kernel, final_norm=final_norm),
        grid=(rows // MLP_TM, D_FF // MLP_TF),
        in_specs=[
            pl.BlockSpec((MLP_TM, D_MODEL), lambda i, f: (i, 0)),
            pl.BlockSpec((1, D_MODEL), lambda i, f: (0, 0)),
            pl.BlockSpec((D_MODEL, MLP_TF), lambda i, f: (0, f)),
            pl.BlockSpec((MLP_TF, D_MODEL), lambda i, f: (f, 0)),
            pl.BlockSpec((1, D_MODEL), lambda i, f: (0, 0)),
        ],
        out_specs=pl.BlockSpec((MLP_TM, D_MODEL), lambda i, f: (i, 0)),
        out_shape=jax.ShapeDtypeStruct(x2d.shape, F32),
        scratch_shapes=[pltpu.VMEM((MLP_TM, D_MODEL), BF16)],
        compiler_params=pltpu.CompilerParams(
            dimension_semantics=("parallel", "arbitrary"), vmem_limit_bytes=vmem),
        name="mlp",
    )(x2d, g, w1, w2, gf)


def _swap_perm(n_a, n_b):
    p = np.zeros((n_a * n_b, n_a * n_b), np.float32)
    a, b = np.meshgrid(np.arange(n_a), np.arange(n_b), indexing="ij")
    p[(b * n_a + a).ravel(), (a * n_b + b).ravel()] = 1.0
    return jnp.asarray(p, dtype=BF16)


LRU_TS = 64
LRU_TN = 1024
LRU_CORE_TS = 128
LRU_CORE_CT = LRU_BLOCK_W
LRU_OUT_TS = 32
LRU_OUT_GROUP = 16


def _lru_in_kernel(x_ref, g_ref, p_ref, w_ref, o_ref, h_ref):
    @pl.when(pl.program_id(1) == 0)
    def _():
        grp = BATCH * SUBLANES
        for sg in range(LRU_TS // SUBLANES):
            x = x_ref[:, sg * SUBLANES:(sg + 1) * SUBLANES, :].reshape(grp, D_MODEL)
            hp = jnp.dot(p_ref[...], _rms_rows(x, g_ref[...]), preferred_element_type=F32)
            h_ref[sg * grp:(sg + 1) * grp, :] = hp.astype(BF16)

    res = jnp.dot(h_ref[...], w_ref[...], preferred_element_type=F32)
    o_ref[...] = res.reshape(LRU_TS, BATCH, LRU_TN).astype(o_ref.dtype)


def _lru_in(x, g, w_in, perm):
    n_out = 2 * D_RNN
    vmem = _vmem_limit(
        2 * _nbytes((BATCH, LRU_TS, D_MODEL), F32), 2 * _nbytes((D_MODEL, LRU_TN), BF16),
        2 * _nbytes((LRU_TS, BATCH, LRU_TN), BF16), _nbytes((LRU_TS * BATCH, D_MODEL), BF16))
    return pl.pallas_call(
        _lru_in_kernel,
        grid=(SEQ // LRU_TS, n_out // LRU_TN),
        in_specs=[
            pl.BlockSpec((BATCH, LRU_TS, D_MODEL), lambda i, j: (0, i, 0)),
            pl.BlockSpec((1, D_MODEL), lambda i, j: (0, 0)),
            pl.BlockSpec(perm.shape, lambda i, j: (0, 0)),
            pl.BlockSpec((D_MODEL, LRU_TN), lambda i, j: (0, j)),
        ],
        out_specs=pl.BlockSpec((LRU_TS, BATCH, LRU_TN), lambda i, j: (i, 0, j)),
        out_shape=jax.ShapeDtypeStruct((SEQ, BATCH, n_out), BF16),
        scratch_shapes=[pltpu.VMEM((LRU_TS * BATCH, D_MODEL), BF16)],
        compiler_params=pltpu.CompilerParams(
            dimension_semantics=("parallel", "arbitrary"), vmem_limit_bytes=vmem),
        name="lru_in",
    )(x, g, perm, w_in)


def _lru_core_kernel(xb_ref, gb_ref, cw_ref, cb_ref, wa_ref, ba_ref, wx_ref, bx_ref, lam_ref,
                     y_ref, ext_ref, a_ref, u_ref, h_ref):
    ts, ct = LRU_CORE_TS, LRU_CORE_CT
    pad = CONV_W - 1

    @pl.when(pl.program_id(1) == 0)
    def _():
        ext_ref[0:pad] = jnp.zeros((pad, BATCH, ct), F32)
        h_ref[...] = jnp.zeros((BATCH, ct), F32)

    ext_ref[pad:pad + ts] = xb_ref[...].astype(F32)
    xc = cb_ref[...].reshape(1, 1, ct) + sum(
        cw_ref[k:k + 1, :].reshape(1, 1, ct) * ext_ref[k:k + ts] for k in range(CONV_W))
    ext_ref[0:pad] = ext_ref[ts:ts + pad]

    xc2 = xc.reshape(ts * BATCH, ct)
    xcb = xc2.astype(BF16)
    r = jax.nn.sigmoid(jnp.dot(xcb, wa_ref[...], preferred_element_type=F32) + ba_ref[...])
    i = jax.nn.sigmoid(jnp.dot(xcb, wx_ref[...], preferred_element_type=F32) + bx_ref[...])
    z = -lam_ref[...]
    softplus = jnp.maximum(z, 0.0) + jnp.log1p(jnp.exp(-jnp.abs(z)))
    log_a = (-LRU_C) * r * softplus
    a_ref[...] = jnp.exp(log_a).reshape(ts, BATCH, ct)
    th = jnp.tanh(log_a)
    u_ref[...] = (xc2 * i * jnp.sqrt(-2.0 * th / (1.0 - th))).reshape(ts, BATCH, ct)

    def step(s, h):
        h = a_ref[s] * h + u_ref[s]
        u_ref[s] = h
        return h
    h_ref[...] = lax.fori_loop(0, ts, step, h_ref[...], unroll=8)

    y_ref[...] = (u_ref[...] * jax.nn.gelu(gb_ref[...].astype(F32))).astype(y_ref.dtype)


def _lru_core(xg, conv_w, conv_b, w_a, b_a, w_x, b_x, lam):
    ts, ct = LRU_CORE_TS, LRU_CORE_CT
    n_cb = D_RNN // ct
    slab = _nbytes((ts, BATCH, ct), F32)
    vmem = _vmem_limit(4 * _nbytes((ts, BATCH, ct), BF16), 2 * _nbytes((ts, BATCH, ct), BF16),
                       4 * _nbytes((ct, ct), BF16), 3 * slab, 8 * slab)
    vec = lambda: pl.BlockSpec((1, ct), lambda c, t: (0, c))
    return pl.pallas_call(
        _lru_core_kernel,
        grid=(n_cb, SEQ // ts),
        in_specs=[
            pl.BlockSpec((ts, BATCH, ct), lambda c, t: (t, 0, c)),
            pl.BlockSpec((ts, BATCH, ct), lambda c, t: (t, 0, c + n_cb)),
            pl.BlockSpec((CONV_W, ct), lambda c, t: (0, c)),
            vec(),
            pl.BlockSpec((None, ct, ct), lambda c, t: (c, 0, 0)),
            vec(),
            pl.BlockSpec((None, ct, ct), lambda c, t: (c, 0, 0)),
            vec(),
            vec(),
        ],
        out_specs=pl.BlockSpec((ts, BATCH, ct), lambda c, t: (t, 0, c)),
        out_shape=jax.ShapeDtypeStruct((SEQ, BATCH, D_RNN), BF16),
        scratch_shapes=[
            pltpu.VMEM((ts + CONV_W - 1, BATCH, ct), F32),
            pltpu.VMEM((ts, BATCH, ct), F32),
            pltpu.VMEM((ts, BATCH, ct), F32),
            pltpu.VMEM((BATCH, ct), F32),
        ],
        compiler_params=pltpu.CompilerParams(
            dimension_semantics=("parallel", "arbitrary"), vmem_limit_bytes=vmem),
        name="lru_core",
    )(xg, xg, conv_w, conv_b, w_a, b_a, w_x, b_x, lam)


def _lru_out_kernel(y_ref, p_ref, w_ref, x_ref, o_ref, l_ref):
    ts, grp = LRU_OUT_TS, LRU_OUT_GROUP
    for g in range(ts // grp):
        yg = y_ref[g * grp:(g + 1) * grp].reshape(grp * BATCH, D_RNN)
        yp = jnp.dot(p_ref[...], yg, preferred_element_type=F32).astype(BF16)
        for b in range(BATCH):
            l_ref[b, g * grp:(g + 1) * grp, :] = yp[b * grp:(b + 1) * grp]
    res = jnp.dot(l_ref[...].reshape(BATCH * ts, D_RNN), w_ref[...], preferred_element_type=F32)
    o_ref[...] = x_ref[...] + res.reshape(BATCH, ts, D_MODEL)


def _lru_out(y, w_out, x, perm):
    ts = LRU_OUT_TS
    vmem = _vmem_limit(
        3 * _nbytes((ts, BATCH, D_RNN), BF16), 2 * _nbytes((D_RNN, D_MODEL), BF16),
        4 * _nbytes((BATCH, ts, D_MODEL), F32))
    xblk = pl.BlockSpec((BATCH, ts, D_MODEL), lambda i: (0, i, 0))
    return pl.pallas_call(
        _lru_out_kernel,
        grid=(SEQ // ts,),
        in_specs=[
            pl.BlockSpec((ts, BATCH, D_RNN), lambda i: (i, 0, 0)),
            pl.BlockSpec(perm.shape, lambda i: (0, 0)),
            pl.BlockSpec((D_RNN, D_MODEL), lambda i: (0, 0)),
            xblk,
        ],
        out_specs=xblk,
        out_shape=jax.ShapeDtypeStruct(x.shape, F32),
        scratch_shapes=[pltpu.VMEM((BATCH, ts, D_RNN), BF16)],
        compiler_params=pltpu.CompilerParams(
            dimension_semantics=("parallel",), vmem_limit_bytes=vmem),
        name="lru_out",
    )(y, perm, w_out, x)


ATT_TM = 1024
ATT_JB = ATT_TM // N_CLASSES
PERM_GROUP = N_CLASSES * N_CLASSES
QKV_TN = 1024
ATT_HEADS = 2
AO_TN = 1024


def _rope_tables():
    inv = ROPE_THETA ** (-jnp.arange(0, ROT_DIM, 2, dtype=F32) / ROT_DIM)
    pos = (jnp.arange(CLASS_LEN, dtype=F32)[None, :] * N_CLASSES
           + jnp.arange(N_CLASSES, dtype=F32)[:, None])
    ang = pos[..., None] * inv
    cos, sin = jnp.cos(ang), jnp.sin(ang)
    ones = jnp.ones(pos.shape + (HEAD_DIM - ROT_DIM,), F32)
    zeros_h = jnp.zeros_like(cos)
    zeros_t = jnp.zeros_like(ones)
    c = jnp.concatenate([cos, cos, ones], axis=-1)
    s_up = jnp.concatenate([-sin, zeros_h, zeros_t], axis=-1)
    s_dn = jnp.concatenate([zeros_h, sin, zeros_t], axis=-1)
    return c, s_up, s_dn


def _qkv_kernel(x_ref, g_ref, p_ref, w_ref, c_ref, su_ref, sd_ref, o_ref, h_ref):
    n = pl.program_id(2)

    @pl.when(n == 0)
    def _():
        for g in range(ATT_TM // PERM_GROUP):
            x = x_ref[g * PERM_GROUP:(g + 1) * PERM_GROUP, :]
            hp = jnp.dot(p_ref[...], _rms_rows(x, g_ref[...]), preferred_element_type=F32).astype(BF16)
            for r in range(N_CLASSES):
                h_ref[r, g * N_CLASSES:(g + 1) * N_CLASSES, :] = hp[r * N_CLASSES:(r + 1) * N_CLASSES]

    res = jnp.dot(h_ref[...].reshape(ATT_TM, D_MODEL), w_ref[...], preferred_element_type=F32)

    @pl.when(n < 2 * D_MODEL // QKV_TN)
    def _():
        c = c_ref[...].reshape(ATT_TM, HEAD_DIM)
        su = su_ref[...].reshape(ATT_TM, HEAD_DIM)
        sd = sd_ref[...].reshape(ATT_TM, HEAD_DIM)
        for hh in range(QKV_TN // HEAD_DIM):
            t = res[:, hh * HEAD_DIM:(hh + 1) * HEAD_DIM]
            up = pltpu.roll(t, HEAD_DIM - ROT_DIM // 2, axis=1)
            dn = pltpu.roll(t, ROT_DIM // 2, axis=1)
            o_ref[:, :, hh * HEAD_DIM:(hh + 1) * HEAD_DIM] = (
                t * c + up * su + dn * sd).reshape(N_CLASSES, ATT_JB, HEAD_DIM)

    @pl.when(n >= 2 * D_MODEL // QKV_TN)
    def _():
        o_ref[...] = res.reshape(N_CLASSES, ATT_JB, QKV_TN)


def _qkv(x, g, w_qkv, tables, perm):
    n_out = 3 * D_MODEL
    tab = pl.BlockSpec((N_CLASSES, ATT_JB, HEAD_DIM), lambda b, m, n: (0, m, 0))
    vmem = _vmem_limit(
        2 * _nbytes((ATT_TM, D_MODEL), F32), 2 * _nbytes((D_MODEL, QKV_TN), BF16),
        2 * _nbytes((ATT_TM, QKV_TN), F32), _nbytes((ATT_TM, D_MODEL), BF16),
        6 * _nbytes((ATT_TM, HEAD_DIM), F32))
    return pl.pallas_call(
        _qkv_kernel,
        grid=(BATCH, SEQ // ATT_TM, n_out // QKV_TN),
        in_specs=[
            pl.BlockSpec((None, ATT_TM, D_MODEL), lambda b, m, n: (b, m, 0)),
            pl.BlockSpec((1, D_MODEL), lambda b, m, n: (0, 0)),
            pl.BlockSpec(perm.shape, lambda b, m, n: (0, 0)),
            pl.BlockSpec((D_MODEL, QKV_TN), lambda b, m, n: (0, n)),
            tab, tab, tab,
        ],
        out_specs=pl.BlockSpec((None, N_CLASSES, ATT_JB, QKV_TN), lambda b, m, n: (b, 0, m, n)),
        out_shape=jax.ShapeDtypeStruct((BATCH, N_CLASSES, CLASS_LEN, n_out), F32),
        scratch_shapes=[pltpu.VMEM((N_CLASSES, ATT_JB, D_MODEL), BF16)],
        compiler_params=pltpu.CompilerParams(
            dimension_semantics=("parallel", "parallel", "arbitrary"), vmem_limit_bytes=vmem),
        name="attn_qkv",
    )(x, g, perm, w_qkv, *tables)


def _tile_order(kind):
    a = lax.broadcasted_iota(jnp.int32, (SUB_BLOCK, SUB_BLOCK), 0)
    b = lax.broadcasted_iota(jnp.int32, (SUB_BLOCK, SUB_BLOCK), 1)
    if kind == 1:
        f = lambda v: N_CLASSES * (v % SUBLANES) + v // SUBLANES
    elif kind == 4:
        f = lambda v: 4 * (v % 32) + v // 32
    else:
        f = lambda v: v
    return f(a), f(b)


def _softmax_tile(q, ks, vs, masks):
    scale = HEAD_DIM ** -0.5
    ss = []
    for k, mk in zip(ks, masks):
        s = lax.dot_general(q, k, (((1,), (1,)), ((), ())), preferred_element_type=F32) * scale
        ss.append(jnp.where(mk, s, NEG_INF))
    m = functools.reduce(jnp.maximum, ss).max(axis=-1, keepdims=True)
    ps = [jnp.exp(s - m) for s in ss]
    l = functools.reduce(jnp.add, ps).sum(axis=-1, keepdims=True)
    acc = functools.reduce(jnp.add, [
        jnp.dot(p.astype(BF16), v, preferred_element_type=F32) for p, v in zip(ps, vs)])
    return m, l, acc


def _merge(m0, l0, acc0, m1, l1, acc1):
    m = jnp.maximum(m0, m1)
    a0 = jnp.exp(m0 - m)
    a1 = jnp.exp(m1 - m)
    return m, a0 * l0 + a1 * l1, a0 * acc0 + a1 * acc1


def _attn_kernel(q_ref, k_ref, v_ref, o_ref, acc_ref, m_ref, l_ref):
    ua1, ub1 = _tile_order(1)
    ua4, ub4 = _tile_order(4)
    ua16, ub16 = _tile_order(16)
    cur1, cur4, cur16 = ub1 <= ua1, ub4 <= ua4, ub16 <= ua16
    heads = [slice(hh * HEAD_DIM, (hh + 1) * HEAD_DIM) for hh in range(ATT_HEADS)]
    bf = lambda t: t.astype(BF16)

    def d16(r, carry):
        ins = [(bf(q_ref[r, :, cs]), bf(k_ref[r, :, cs]), bf(v_ref[r, :, cs])) for cs in heads]
        outs = [_softmax_tile(q, [k], [v], [cur16]) for q, k, v in ins]
        for hh, (m, l, acc) in enumerate(outs):
            m_ref[hh, r] = m
            l_ref[hh, r] = l
            acc_ref[hh, r] = acc
        return carry
    lax.fori_loop(0, N_CLASSES, d16, 0, unroll=2)

    def gather4(ref, lead, r4, start, cols):
        return jnp.concatenate(
            [ref[lead + (r4 + 4 * c, pl.ds(start, 32), cols)] for c in range(4)], axis=0)

    def scatter4(ref, hh, r4, start, val):
        for c in range(4):
            ref[hh, r4 + 4 * c, pl.ds(start, 32), :] = val[32 * c:32 * (c + 1)]

    def d4(n4, carry):
        start = pl.multiple_of(n4 * 32, 32)
        prev = pl.multiple_of(jnp.maximum(n4 - 1, 0) * 32, 32)
        prev4 = ub4 >= ua4 + jnp.where(n4 > 0, 0, SUB_BLOCK)
        work = []
        for r4 in range(4):
            for hh, cs in enumerate(heads):
                ins = (bf(gather4(q_ref, (), r4, start, cs)),
                       [bf(gather4(k_ref, (), r4, prev, cs)), bf(gather4(k_ref, (), r4, start, cs))],
                       [bf(gather4(v_ref, (), r4, prev, cs)), bf(gather4(v_ref, (), r4, start, cs))])
                old = (gather4(m_ref, (hh,), r4, start, slice(None)),
                       gather4(l_ref, (hh,), r4, start, slice(None)),
                       gather4(acc_ref, (hh,), r4, start, slice(None)))
                work.append((hh, r4, ins, old))
        done = []
        for hh, r4, (q, ks, vs), old in work:
            done.append((hh, r4, _merge(*old, *_softmax_tile(q, ks, vs, [prev4, cur4]))))
        for hh, r4, (m, l, acc) in done:
            scatter4(m_ref, hh, r4, start, m)
            scatter4(l_ref, hh, r4, start, l)
            scatter4(acc_ref, hh, r4, start, acc)
        return carry
    lax.fori_loop(0, CLASS_LEN // 32, d4, 0)

    rows = N_CLASSES * SUBLANES

    def d1(n2, carry):
        start = pl.multiple_of(n2 * 2 * SUBLANES, 2 * SUBLANES)
        prev = pl.multiple_of(jnp.maximum(n2 * 2 - 1, 0) * SUBLANES, SUBLANES)
        prev1 = ub1 >= ua1
        prev1_first = ub1 >= ua1 + jnp.where(n2 > 0, 0, SUB_BLOCK)
        pair = pl.ds(start, 2 * SUBLANES)
        lo, hi = slice(0, SUBLANES), slice(SUBLANES, 2 * SUBLANES)
        flat = lambda t: t.reshape(rows, t.shape[-1])
        work = []
        for hh, cs in enumerate(heads):
            qb, kb, vb = q_ref[:, pair, cs], k_ref[:, pair, cs], v_ref[:, pair, cs]
            kp, vp = k_ref[:, pl.ds(prev, SUBLANES), cs], v_ref[:, pl.ds(prev, SUBLANES), cs]
            mb, lb, ab = m_ref[hh, :, pair, :], l_ref[hh, :, pair, :], acc_ref[hh, :, pair, :]
            for half, kprev, vprev, pmask in ((lo, kp, vp, prev1_first), (hi, kb[:, lo], vb[:, lo], prev1)):
                ins = (bf(flat(qb[:, half])),
                       [bf(flat(kprev)), bf(flat(kb[:, half]))],
                       [bf(flat(vprev)), bf(flat(vb[:, half]))], [pmask, cur1])
                old = (flat(mb[:, half]), flat(lb[:, half]), flat(ab[:, half]))
                work.append((ins, old))
        outs = []
        for (q, ks, vs, masks), old in work:
            _, l, acc = _merge(*old, *_softmax_tile(q, ks, vs, masks))
            outs.append((acc / l).reshape(N_CLASSES, SUBLANES, HEAD_DIM))
        for hh, cs in enumerate(heads):
            o_ref[:, pair, cs] = jnp.concatenate(outs[2 * hh:2 * hh + 2], axis=1).astype(o_ref.dtype)
        return carry
    lax.fori_loop(0, CLASS_LEN // (2 * SUBLANES), d1, 0)


def _attn(qkv):
    blk = (None, N_CLASSES, CLASS_LEN, ATT_HEADS * HEAD_DIM)
    n_hg = N_HEADS // ATT_HEADS
    tile = _nbytes((N_CLASSES, CLASS_LEN, ATT_HEADS * HEAD_DIM), F32)
    state = ATT_HEADS * _nbytes((N_CLASSES, CLASS_LEN, HEAD_DIM), F32)
    vmem = _vmem_limit(6 * tile, tile, 3 * state)
    return pl.pallas_call(
        _attn_kernel,
        grid=(BATCH, n_hg),
        in_specs=[
            pl.BlockSpec(blk, lambda b, g: (b, 0, 0, g)),
            pl.BlockSpec(blk, lambda b, g: (b, 0, 0, n_hg + g)),
            pl.BlockSpec(blk, lambda b, g: (b, 0, 0, 2 * n_hg + g)),
        ],
        out_specs=pl.BlockSpec(blk, lambda b, g: (b, 0, 0, g)),
        out_shape=jax.ShapeDtypeStruct((BATCH, N_CLASSES, CLASS_LEN, D_MODEL), BF16),
        scratch_shapes=[
            pltpu.VMEM((ATT_HEADS, N_CLASSES, CLASS_LEN, HEAD_DIM), F32),
            pltpu.VMEM((ATT_HEADS, N_CLASSES, CLASS_LEN, 1), F32),
            pltpu.VMEM((ATT_HEADS, N_CLASSES, CLASS_LEN, 1), F32),
        ],
        compiler_params=pltpu.CompilerParams(
            dimension_semantics=("parallel", "parallel"), vmem_limit_bytes=vmem),
        name="attn_core",
    )(qkv, qkv, qkv)


def _attn_out_kernel(a_ref, p_ref, w_ref, x_ref, o_ref, l_ref):
    @pl.when(pl.program_id(2) == 0)
    def _():
        for g in range(ATT_TM // PERM_GROUP):
            ag = a_ref[:, g * N_CLASSES:(g + 1) * N_CLASSES, :].reshape(PERM_GROUP, D_MODEL)
            l_ref[g * PERM_GROUP:(g + 1) * PERM_GROUP, :] = jnp.dot(
                p_ref[...], ag, preferred_element_type=F32).astype(BF16)

    o_ref[...] = x_ref[...] + jnp.dot(l_ref[...], w_ref[...], preferred_element_type=F32)


def _attn_out(att, w_o, x, perm):
    xblk = pl.BlockSpec((None, ATT_TM, AO_TN), lambda b, m, n: (b, m, n))
    vmem = _vmem_limit(
        3 * _nbytes((ATT_TM, D_MODEL), BF16), 2 * _nbytes((D_MODEL, AO_TN), BF16),
        4 * _nbytes((ATT_TM, AO_TN), F32))
    return pl.pallas_call(
        _attn_out_kernel,
        grid=(BATCH, SEQ // ATT_TM, D_MODEL // AO_TN),
        in_specs=[
            pl.BlockSpec((None, N_CLASSES, ATT_JB, D_MODEL), lambda b, m, n: (b, 0, m, 0)),
            pl.BlockSpec(perm.shape, lambda b, m, n: (0, 0)),
            pl.BlockSpec((D_MODEL, AO_TN), lambda b, m, n: (0, n)),
            xblk,
        ],
        out_specs=xblk,
        out_shape=jax.ShapeDtypeStruct(x.shape, F32),
        scratch_shapes=[pltpu.VMEM((ATT_TM, D_MODEL), BF16)],
        compiler_params=pltpu.CompilerParams(
            dimension_semantics=("parallel", "parallel", "arbitrary"), vmem_limit_bytes=vmem),
        name="attn_out",
    )(att, perm, w_o, x)


def kernel(x, mix_norm, mlp_norm, final_norm, mlp_w1, mlp_w2,
           lru_w_in, lru_conv_w, lru_conv_b, lru_w_a, lru_b_a, lru_w_x, lru_b_x,
           lru_lambda, lru_w_out, attn_w_qkv, attn_w_o):
    assert x.shape == (BATCH, SEQ, D_MODEL) and x.dtype == F32
    row = lambda v: v.reshape(1, -1)
    tables = _rope_tables()
    perm_bt = _swap_perm(BATCH, SUBLANES)
    perm_sq = _swap_perm(N_CLASSES, N_CLASSES)
    gf = row(final_norm)
    for i in range(DEPTH):
        j = i // N_MIXERS
        g = row(mix_norm[i])
        if i % N_MIXERS == 0:
            xg = _lru_in(x, g, lru_w_in[j].astype(BF16), perm_bt)
            y = _lru_core(xg, lru_conv_w[j], row(lru_conv_b[j]),
                          lru_w_a[j].astype(BF16), row(lru_b_a[j]),
                          lru_w_x[j].astype(BF16), row(lru_b_x[j]), row(lru_lambda[j]))
            x = _lru_out(y, lru_w_out[j].astype(BF16), x, perm_sq)
        else:
            qkv = _qkv(x, g, attn_w_qkv[j].astype(BF16), tables, perm_sq)
            att = _attn(qkv)
            x = _attn_out(att, attn_w_o[j].astype(BF16), x, perm_sq)
        x2d = _mlp(x.reshape(BATCH * SEQ, D_MODEL), row(mlp_norm[i]),
                   mlp_w1[i].astype(BF16), mlp_w2[i].astype(BF16), gf, i == DEPTH - 1)
        x = x2d.reshape(BATCH, SEQ, D_MODEL)
    return x
```

```python
import functools
import math

import jax
import jax.numpy as jnp
import numpy as np
from jax import lax
from jax.experimental import pallas as pl
from jax.experimental.pallas import tpu as pltpu

D_MODEL = 2048
BATCH = 16
SEQ = 2048
DEPTH = 4
N_MIXERS = 2
D_RNN = 2560
LRU_BLOCKS = 10
LRU_BLOCK_W = D_RNN // LRU_BLOCKS
CONV_W = 4
LRU_C = 8.0
N_HEADS = 16
HEAD_DIM = D_MODEL // N_HEADS
ROT_DIM = HEAD_DIM // 4
ROPE_THETA = 500000.0
DILATED_PAIRS = ((128, 1), (512, 4), (2048, 16))
SUB_BLOCK = 128
D_FF = 4 * D_MODEL
EPS = 1e-6
NEG_INF = -1e30

V7X_VMEM_BYTES = 64 * 1024 * 1024
V7X_VMEM_RESERVED_BYTES = 4 * 1024 * 1024
MOSAIC_TEMP_BYTES = 10 * 1024 * 1024
SUBLANES = 8
LANES = 128

N_CLASSES = 16
CLASS_LEN = SEQ // N_CLASSES

F32 = jnp.float32
BF16 = jnp.bfloat16
LOG2_E = math.log2(math.e)
GELU_K = math.sqrt(2.0 / math.pi)


def _vmem_limit(*buffer_bytes):
    need = sum(buffer_bytes) + MOSAIC_TEMP_BYTES
    return int(min(need, V7X_VMEM_BYTES - V7X_VMEM_RESERVED_BYTES))


def _nbytes(shape, dtype):
    return int(np.prod(shape)) * jnp.dtype(dtype).itemsize


def _rms_rows(x, g):
    ms = jnp.mean(x * x, axis=-1, keepdims=True)
    return (x * lax.rsqrt(ms + EPS) * g).astype(BF16)


MLP_TM = 1024
MLP_TF = 512
NORM_CHUNK = 128


def _mlp_kernel(x_ref, g_ref, w1_ref, w2_ref, gf_ref, o_ref, h_ref, *, final_norm):
    f = pl.program_id(1)

    @pl.when(f == 0)
    def _():
        def body(c, carry):
            rows = pl.ds(pl.multiple_of(c * NORM_CHUNK, NORM_CHUNK), NORM_CHUNK)
            x = x_ref[rows, :]
            h_ref[rows, :] = _rms_rows(x, g_ref[...])
            o_ref[rows, :] = x
            return carry
        lax.fori_loop(0, MLP_TM // NORM_CHUNK, body, 0)

    u = jnp.dot(h_ref[...], w1_ref[...], preferred_element_type=F32)
    u = jnp.square(jnp.maximum(u, 0.0)).astype(BF16)
    o_ref[...] += jnp.dot(u, w2_ref[...], preferred_element_type=F32)

    if final_norm:
        @pl.when(f == pl.num_programs(1) - 1)
        def _():
            def body(c, carry):
                rows = pl.ds(pl.multiple_of(c * NORM_CHUNK, NORM_CHUNK), NORM_CHUNK)
                y = o_ref[rows, :]
                ms = jnp.mean(y * y, axis=-1, keepdims=True)
                o_ref[rows, :] = y * lax.rsqrt(ms + EPS) * gf_ref[...]
                return carry
            lax.fori_loop(0, MLP_TM // NORM_CHUNK, body, 0)


def _mlp(x2d, g, w1, w2, gf, final_norm, layer):
    rows = x2d.shape[0]
    vmem = _vmem_limit(
        2 * _nbytes((MLP_TM, D_MODEL), F32), 2 * _nbytes((MLP_TM, D_MODEL), F32),
        _nbytes((MLP_TM, D_MODEL), BF16),
        2 * _nbytes((D_MODEL, MLP_TF), BF16), 2 * _nbytes((MLP_TF, D_MODEL), BF16))
    return pl.pallas_call(
        functools.partial(_mlp_kernel, final_norm=final_norm),
        grid=(rows // MLP_TM, D_FF // MLP_TF),
        in_specs=[
            pl.BlockSpec((MLP_TM, D_MODEL), lambda i, f: (i, 0)),
            pl.BlockSpec((1, D_MODEL), lambda i, f: (0, 0)),
            pl.BlockSpec((None, D_MODEL, MLP_TF), lambda i, f: (layer, 0, f)),
            pl.BlockSpec((None, MLP_TF, D_MODEL), lambda i, f: (layer, f, 0)),
            pl.BlockSpec((1, D_MODEL), lambda i, f: (0, 0)),
        ],
        out_specs=pl.BlockSpec((MLP_TM, D_MODEL), lambda i, f: (i, 0)),
        out_shape=jax.ShapeDtypeStruct(x2d.shape, F32),
        scratch_shapes=[pltpu.VMEM((MLP_TM, D_MODEL), BF16)],
        compiler_params=pltpu.CompilerParams(
            dimension_semantics=("parallel", "arbitrary"), vmem_limit_bytes=vmem),
        name="mlp",
    )(x2d, g, w1, w2, gf)


def _swap_perm(n_a, n_b):
    p = np.zeros((n_a * n_b, n_a * n_b), np.float32)
    a, b = np.meshgrid(np.arange(n_a), np.arange(n_b), indexing="ij")
    p[(b * n_a + a).ravel(), (a * n_b + b).ravel()] = 1.0
    return jnp.asarray(p, dtype=BF16)


LRU_TS = 64
LRU_TN = 1024
LRU_CORE_TS = 128
LRU_CORE_CT = LRU_BLOCK_W
LRU_OUT_TS = 32
LRU_OUT_GROUP = 16


def _lru_in_kernel(x_ref, g_ref, p_ref, w_ref, o_ref, h_ref):
    @pl.when(pl.program_id(1) == 0)
    def _():
        grp = BATCH * SUBLANES
        for sg in range(LRU_TS // SUBLANES):
            x = x_ref[:, sg * SUBLANES:(sg + 1) * SUBLANES, :].reshape(grp, D_MODEL)
            hp = jnp.dot(p_ref[...], _rms_rows(x, g_ref[...]), preferred_element_type=F32)
            h_ref[sg * grp:(sg + 1) * grp, :] = hp.astype(BF16)

    res = jnp.dot(h_ref[...], w_ref[...], preferred_element_type=F32)
    o_ref[...] = res.reshape(LRU_TS, BATCH, LRU_TN).astype(o_ref.dtype)


def _lru_in(x, g, w_in, perm, layer):
    n_out = 2 * D_RNN
    vmem = _vmem_limit(
        2 * _nbytes((BATCH, LRU_TS, D_MODEL), F32), 2 * _nbytes((D_MODEL, LRU_TN), BF16),
        2 * _nbytes((LRU_TS, BATCH, LRU_TN), BF16), _nbytes((LRU_TS * BATCH, D_MODEL), BF16))
    return pl.pallas_call(
        _lru_in_kernel,
        grid=(SEQ // LRU_TS, n_out // LRU_TN),
        in_specs=[
            pl.BlockSpec((BATCH, LRU_TS, D_MODEL), lambda i, j: (0, i, 0)),
            pl.BlockSpec((1, D_MODEL), lambda i, j: (0, 0)),
            pl.BlockSpec(perm.shape, lambda i, j: (0, 0)),
            pl.BlockSpec((None, D_MODEL, LRU_TN), lambda i, j: (layer, 0, j)),
        ],
        out_specs=pl.BlockSpec((LRU_TS, BATCH, LRU_TN), lambda i, j: (i, 0, j)),
        out_shape=jax.ShapeDtypeStruct((SEQ, BATCH, n_out), BF16),
        scratch_shapes=[pltpu.VMEM((LRU_TS * BATCH, D_MODEL), BF16)],
        compiler_params=pltpu.CompilerParams(
            dimension_semantics=("parallel", "arbitrary"), vmem_limit_bytes=vmem),
        name="lru_in",
    )(x, g, perm, w_in)


def _lru_core_kernel(xb_ref, gb_ref, cw_ref, cb_ref, wa_ref, ba_ref, wx_ref, bx_ref, lam_ref,
                     y_ref, ext_ref, a_ref, u_ref, h_ref):
    ts, ct = LRU_CORE_TS, LRU_CORE_CT
    pad = CONV_W - 1

    @pl.when(pl.program_id(1) == 0)
    def _():
        ext_ref[0:pad] = jnp.zeros((pad, BATCH, ct), F32)
        h_ref[...] = jnp.zeros((BATCH, ct), F32)

    ext_ref[pad:pad + ts] = xb_ref[...].astype(F32)
    xc = cb_ref[...].reshape(1, 1, ct) + sum(
        cw_ref[k:k + 1, :].reshape(1, 1, ct) * ext_ref[k:k + ts] for k in range(CONV_W))
    ext_ref[0:pad] = ext_ref[ts:ts + pad]

    xc2 = xc.reshape(ts * BATCH, ct)
    xcb = xc2.astype(BF16)
    sigmoid = lambda t: 1.0 / (1.0 + jnp.exp2(t * (-LOG2_E)))
    r = sigmoid(jnp.dot(xcb, wa_ref[...], preferred_element_type=F32) + ba_ref[...])
    i = sigmoid(jnp.dot(xcb, wx_ref[...], preferred_element_type=F32) + bx_ref[...])
    z = -lam_ref[...]
    softplus = jnp.maximum(z, 0.0) + jnp.log1p(jnp.exp(-jnp.abs(z)))
    log_a = r * ((-LRU_C) * softplus)
    a_ref[...] = jnp.exp(log_a).reshape(ts, BATCH, ct)
    th = jnp.tanh(log_a)
    gap = -2.0 * th / (1.0 - th)
    mult = jnp.where(gap > 0.0, gap * lax.rsqrt(gap), 0.0)
    u_ref[...] = (xc2 * i * mult).reshape(ts, BATCH, ct)

    def step(s, h):
        h = a_ref[s] * h + u_ref[s]
        u_ref[s] = h
        return h
    h_ref[...] = lax.fori_loop(0, ts, step, h_ref[...], unroll=8)

    gb = gb_ref[...].astype(F32)
    inner = gb * (GELU_K + (0.044715 * GELU_K) * (gb * gb))
    y_ref[...] = ((0.5 * u_ref[...]) * (gb * (1.0 + jnp.tanh(inner)))).astype(y_ref.dtype)


def _lru_core(xg, conv_w, conv_b, w_a, b_a, w_x, b_x, lam, layer):
    ts, ct = LRU_CORE_TS, LRU_CORE_CT
    n_cb = D_RNN // ct
    slab = _nbytes((ts, BATCH, ct), F32)
    vmem = _vmem_limit(4 * _nbytes((ts, BATCH, ct), BF16), 2 * _nbytes((ts, BATCH, ct), BF16),
                       4 * _nbytes((ct, ct), BF16), 3 * slab, 8 * slab)
    vec = lambda: pl.BlockSpec((1, ct), lambda c, t: (0, c))
    return pl.pallas_call(
        _lru_core_kernel,
        grid=(n_cb, SEQ // ts),
        in_specs=[
            pl.BlockSpec((ts, BATCH, ct), lambda c, t: (t, 0, c)),
            pl.BlockSpec((ts, BATCH, ct), lambda c, t: (t, 0, c + n_cb)),
            pl.BlockSpec((CONV_W, ct), lambda c, t: (0, c)),
            vec(),
            pl.BlockSpec((None, None, ct, ct), lambda c, t: (layer, c, 0, 0)),
            vec(),
            pl.BlockSpec((None, None, ct, ct), lambda c, t: (layer, c, 0, 0)),
            vec(),
            vec(),
        ],
        out_specs=pl.BlockSpec((ts, BATCH, ct), lambda c, t: (t, 0, c)),
        out_shape=jax.ShapeDtypeStruct((SEQ, BATCH, D_RNN), BF16),
        scratch_shapes=[
            pltpu.VMEM((ts + CONV_W - 1, BATCH, ct), F32),
            pltpu.VMEM((ts, BATCH, ct), F32),
            pltpu.VMEM((ts, BATCH, ct), F32),
            pltpu.VMEM((BATCH, ct), F32),
        ],
        compiler_params=pltpu.CompilerParams(
            dimension_semantics=("parallel", "arbitrary"), vmem_limit_bytes=vmem),
        name="lru_core",
    )(xg, xg, conv_w, conv_b, w_a, b_a, w_x, b_x, lam)


def _lru_out_kernel(y_ref, p_ref, w_ref, x_ref, o_ref, l_ref):
    ts, grp = LRU_OUT_TS, LRU_OUT_GROUP
    for g in range(ts // grp):
        yg = y_ref[g * grp:(g + 1) * grp].reshape(grp * BATCH, D_RNN)
        yp = jnp.dot(p_ref[...], yg, preferred_element_type=F32).astype(BF16)
        for b in range(BATCH):
            l_ref[b, g * grp:(g + 1) * grp, :] = yp[b * grp:(b + 1) * grp]
    res = jnp.dot(l_ref[...].reshape(BATCH * ts, D_RNN), w_ref[...], preferred_element_type=F32)
    o_ref[...] = x_ref[...] + res.reshape(BATCH, ts, D_MODEL)


def _lru_out(y, w_out, x, perm, layer):
    ts = LRU_OUT_TS
    vmem = _vmem_limit(
        3 * _nbytes((ts, BATCH, D_RNN), BF16), 2 * _nbytes((D_RNN, D_MODEL), BF16),
        4 * _nbytes((BATCH, ts, D_MODEL), F32))
    xblk = pl.BlockSpec((BATCH, ts, D_MODEL), lambda i: (0, i, 0))
    return pl.pallas_call(
        _lru_out_kernel,
        grid=(SEQ // ts,),
        in_specs=[
            pl.BlockSpec((ts, BATCH, D_RNN), lambda i: (i, 0, 0)),
            pl.BlockSpec(perm.shape, lambda i: (0, 0)),
            pl.BlockSpec((None, D_RNN, D_MODEL), lambda i: (layer, 0, 0)),
            xblk,
        ],
        out_specs=xblk,
        out_shape=jax.ShapeDtypeStruct(x.shape, F32),
        scratch_shapes=[pltpu.VMEM((BATCH, ts, D_RNN), BF16)],
        compiler_params=pltpu.CompilerParams(
            dimension_semantics=("parallel",), vmem_limit_bytes=vmem),
        name="lru_out",
    )(y, perm, w_out, x)


ATT_TM = 1024
ATT_JB = ATT_TM // N_CLASSES
PERM_GROUP = N_CLASSES * N_CLASSES
QKV_TN = 1024
QKV_CHUNK = 256
ATT_HEADS = 4
AO_TN = 1024
QCHUNKS = 4
QCHUNK = SUB_BLOCK // QCHUNKS
SCORE_SCALE = HEAD_DIM ** -0.5 * math.log2(math.e)


def _rope_tables():
    inv = ROPE_THETA ** (-jnp.arange(0, ROT_DIM, 2, dtype=F32) / ROT_DIM)
    pos = (jnp.arange(CLASS_LEN, dtype=F32)[None, :] * N_CLASSES
           + jnp.arange(N_CLASSES, dtype=F32)[:, None])
    ang = pos[..., None] * inv
    cos, sin = jnp.cos(ang), jnp.sin(ang)
    ones = jnp.ones(pos.shape + (HEAD_DIM - ROT_DIM,), F32)
    zeros_h = jnp.zeros_like(cos)
    zeros_t = jnp.zeros_like(ones)
    c = jnp.concatenate([cos, cos, ones], axis=-1)
    s_up = jnp.concatenate([-sin, zeros_h, zeros_t], axis=-1)
    s_dn = jnp.concatenate([zeros_h, sin, zeros_t], axis=-1)
    ident = (jnp.ones_like(c), jnp.zeros_like(c), jnp.zeros_like(c))
    return tuple(jnp.stack([t * SCORE_SCALE, t, e]) for t, e in zip((c, s_up, s_dn), ident))


def _qkv_kernel(x_ref, g_ref, p_ref, w_ref, c_ref, su_ref, sd_ref, o_ref, h_ref):
    n = pl.program_id(2)

    @pl.when(n == 0)
    def _():
        for g in range(ATT_TM // PERM_GROUP):
            x = x_ref[g * PERM_GROUP:(g + 1) * PERM_GROUP, :]
            hp = jnp.dot(p_ref[...], _rms_rows(x, g_ref[...]), preferred_element_type=F32).astype(BF16)
            for r in range(N_CLASSES):
                h_ref[r, g * N_CLASSES:(g + 1) * N_CLASSES, :] = hp[r * N_CLASSES:(r + 1) * N_CLASSES]

    hmat = h_ref[...].reshape(ATT_TM, D_MODEL)
    c = c_ref[...].reshape(ATT_TM, HEAD_DIM)
    su = su_ref[...].reshape(ATT_TM, HEAD_DIM)
    sd = sd_ref[...].reshape(ATT_TM, HEAD_DIM)
    for ch in range(QKV_TN // QKV_CHUNK):
        res = jnp.dot(hmat, w_ref[:, ch * QKV_CHUNK:(ch + 1) * QKV_CHUNK], preferred_element_type=F32)
        for hh in range(QKV_CHUNK // HEAD_DIM):
            t = res[:, hh * HEAD_DIM:(hh + 1) * HEAD_DIM]
            up = pltpu.roll(t, HEAD_DIM - ROT_DIM // 2, axis=1)
            dn = pltpu.roll(t, ROT_DIM // 2, axis=1)
            col = ch * QKV_CHUNK + hh * HEAD_DIM
            o_ref[:, :, col:col + HEAD_DIM] = (t * c + up * su + dn * sd).reshape(N_CLASSES, ATT_JB, HEAD_DIM)


def _qkv(x, g, w_qkv, tables, perm, layer):
    n_out = 3 * D_MODEL
    q_tiles = D_MODEL // QKV_TN
    tab = pl.BlockSpec((None, N_CLASSES, ATT_JB, HEAD_DIM), lambda b, m, n: (n // q_tiles, 0, m, 0))
    vmem = _vmem_limit(
        2 * _nbytes((ATT_TM, D_MODEL), F32), 2 * _nbytes((D_MODEL, QKV_TN), BF16),
        2 * _nbytes((ATT_TM, QKV_TN), F32), _nbytes((ATT_TM, D_MODEL), BF16),
        6 * _nbytes((ATT_TM, HEAD_DIM), F32))
    return pl.pallas_call(
        _qkv_kernel,
        grid=(BATCH, SEQ // ATT_TM, n_out // QKV_TN),
        in_specs=[
            pl.BlockSpec((None, ATT_TM, D_MODEL), lambda b, m, n: (b, m, 0)),
            pl.BlockSpec((1, D_MODEL), lambda b, m, n: (0, 0)),
            pl.BlockSpec(perm.shape, lambda b, m, n: (0, 0)),
            pl.BlockSpec((None, D_MODEL, QKV_TN), lambda b, m, n: (layer, 0, n)),
            tab, tab, tab,
        ],
        out_specs=pl.BlockSpec((None, N_CLASSES, ATT_JB, QKV_TN), lambda b, m, n: (b, 0, m, n)),
        out_shape=jax.ShapeDtypeStruct((BATCH, N_CLASSES, CLASS_LEN, n_out), F32),
        scratch_shapes=[pltpu.VMEM((N_CLASSES, ATT_JB, D_MODEL), BF16)],
        compiler_params=pltpu.CompilerParams(
            dimension_semantics=("parallel", "parallel", "arbitrary"), vmem_limit_bytes=vmem),
        name="attn_qkv",
    )(x, g, perm, w_qkv, *tables)


def _tile_order(kind):
    a = lax.broadcasted_iota(jnp.int32, (SUB_BLOCK, SUB_BLOCK), 0)
    b = lax.broadcasted_iota(jnp.int32, (SUB_BLOCK, SUB_BLOCK), 1)
    if kind == 1:
        f = lambda v: N_CLASSES * (v % SUBLANES) + v // SUBLANES
    else:
        f = lambda v: QCHUNKS * (v % QCHUNK) + v // QCHUNK
    return f(a), f(b)


def _softmax_tiles(tiles):
    ss = [[jnp.where(mk, _dot_nt(q, k), NEG_INF) for k, mk in zip(ks, masks)] for q, ks, _, masks in tiles]
    ms, ps, ls = _softmax_rows(ss)
    accs = [functools.reduce(jnp.add, [jnp.dot(pb, v, preferred_element_type=F32) for pb, v in zip(p, vs)])
            for p, (_, _, vs, _) in zip(ps, tiles)]
    return list(zip(ms, ls, accs))


def _dot_nt(a, b):
    return lax.dot_general(a, b, (((1,), (1,)), ((), ())), preferred_element_type=F32)


def _softmax_rows(ss):
    full = (SUB_BLOCK, SUB_BLOCK)
    ms = [jnp.broadcast_to(functools.reduce(jnp.maximum, s).max(axis=-1, keepdims=True), full) for s in ss]
    ps = [[jnp.exp2(sb - m) for sb in s] for s, m in zip(ss, ms)]
    ls = [jnp.broadcast_to(functools.reduce(jnp.add, p).sum(axis=-1, keepdims=True), full) for p in ps]
    return ms, [[pb.astype(BF16) for pb in p] for p in ps], ls


def _first_pass(groups, cur4, prev4, causal):
    n_blk = CLASS_LEN // QCHUNK
    rows = lambda t, n: t[n * QCHUNK:(n + 1) * QCHUNK]
    regroup = lambda blocks, n: jnp.concatenate([rows(b, n) for b in blocks], axis=0)
    blocks = range(n_blk)
    qt = [[regroup(q_c, n) for n in blocks] for q_c, _, _ in groups]
    kt = [[regroup(k_c, n) for n in blocks] for _, k_c, _ in groups]
    vt = [[regroup(v_c, n) for n in blocks] for _, _, v_c in groups]
    s16 = [[jnp.where(causal, _dot_nt(q, k), NEG_INF) for q, k in zip(q_c, k_c)] for q_c, k_c, _ in groups]
    ss = []
    for g in range(len(groups)):
        for n in blocks:
            s = [jnp.where(cur4, _dot_nt(qt[g][n], kt[g][n]), NEG_INF), regroup(s16[g], n)]
            if n > 0:
                s.append(jnp.where(prev4, _dot_nt(qt[g][n], kt[g][n - 1]), NEG_INF))
            ss.append(s)
    ms, ps, ls = _softmax_rows(ss)
    out = []
    for g, (_, _, v_c) in enumerate(groups):
        tile = lambda n: g * n_blk + n
        pv16 = [jnp.dot(regroup([ps[tile(n)][1] for n in blocks], c), v_c[c], preferred_element_type=F32)
                for c in range(QCHUNKS)]
        accs = []
        for n in blocks:
            acc = jnp.dot(ps[tile(n)][0], vt[g][n], preferred_element_type=F32) + regroup(pv16, n)
            if n > 0:
                acc = acc + jnp.dot(ps[tile(n)][2], vt[g][n - 1], preferred_element_type=F32)
            accs.append(acc)
        out.append([(regroup([ms[tile(n)] for n in blocks], c), regroup([ls[tile(n)] for n in blocks], c),
                     regroup(accs, c)) for c in range(QCHUNKS)])
    return out


def _merge_all(olds, news):
    ms = [jnp.maximum(o[0], n[0]) for o, n in zip(olds, news)]
    a0 = [jnp.exp2(o[0] - m) for o, m in zip(olds, ms)]
    a1 = [jnp.exp2(n[0] - m) for n, m in zip(news, ms)]
    ls = [x * o[1] + y * n[1] for x, y, o, n in zip(a0, a1, olds, news)]
    accs = [x * o[2] + y * n[2] for x, y, o, n in zip(a0, a1, olds, news)]
    return list(zip(ms, ls, accs))


def _attn_kernel(q_ref, k_ref, v_ref, o_ref, acc_ref, m_ref, l_ref):
    ua1, ub1 = _tile_order(1)
    ua4, ub4 = _tile_order(4)
    cur1, cur4, prev4 = ub1 <= ua1, ub4 <= ua4, ub4 >= ua4
    causal = (lax.broadcasted_iota(jnp.int32, (SUB_BLOCK, SUB_BLOCK), 1)
              <= lax.broadcasted_iota(jnp.int32, (SUB_BLOCK, SUB_BLOCK), 0))
    heads = [slice(hh * HEAD_DIM, (hh + 1) * HEAD_DIM) for hh in range(ATT_HEADS)]
    bf = lambda t: t.astype(BF16)

    def first(r4, carry):
        classes = [r4 + QCHUNKS * c for c in range(QCHUNKS)]
        groups = [tuple([bf(ref[r, :, cs]) for r in classes] for ref in (q_ref, k_ref, v_ref)) for cs in heads]
        for hh, per_class in enumerate(_first_pass(groups, cur4, prev4, causal)):
            for r, (m, l, acc) in zip(classes, per_class):
                m_ref[hh, r] = m
                l_ref[hh, r] = l
                acc_ref[hh, r] = acc
        return carry
    lax.fori_loop(0, QCHUNKS, first, 0)

    rows = N_CLASSES * SUBLANES

    def d1(n2, carry):
        start = pl.multiple_of(n2 * 2 * SUBLANES, 2 * SUBLANES)
        prev = pl.multiple_of(jnp.maximum(n2 * 2 - 1, 0) * SUBLANES, SUBLANES)
        prev1 = ub1 >= ua1
        prev1_first = ub1 >= ua1 + jnp.where(n2 > 0, 0, SUB_BLOCK)
        pair = pl.ds(start, 2 * SUBLANES)
        lo, hi = slice(0, SUBLANES), slice(SUBLANES, 2 * SUBLANES)
        flat = lambda t: t.reshape(rows, t.shape[-1])
        tiles, olds = [], []
        for hh, cs in enumerate(heads):
            qb, kb, vb = q_ref[:, pair, cs], k_ref[:, pair, cs], v_ref[:, pair, cs]
            kp, vp = k_ref[:, pl.ds(prev, SUBLANES), cs], v_ref[:, pl.ds(prev, SUBLANES), cs]
            mb, lb, ab = m_ref[hh, :, pair, :], l_ref[hh, :, pair, :], acc_ref[hh, :, pair, :]
            for half, kprev, vprev, pmask in ((lo, kp, vp, prev1_first), (hi, kb[:, lo], vb[:, lo], prev1)):
                tiles.append((bf(flat(qb[:, half])),
                              [bf(flat(kprev)), bf(flat(kb[:, half]))],
                              [bf(flat(vprev)), bf(flat(vb[:, half]))], [pmask, cur1]))
                olds.append((flat(mb[:, half]), flat(lb[:, half]), flat(ab[:, half])))
        outs = [(acc / l).reshape(N_CLASSES, SUBLANES, HEAD_DIM)
                for _, l, acc in _merge_all(olds, _softmax_tiles(tiles))]
        for hh, cs in enumerate(heads):
            o_ref[:, pair, cs] = jnp.concatenate(outs[2 * hh:2 * hh + 2], axis=1).astype(o_ref.dtype)
        return carry
    lax.fori_loop(0, CLASS_LEN // (2 * SUBLANES), d1, 0)


def _attn(qkv):
    blk = (None, N_CLASSES, CLASS_LEN, ATT_HEADS * HEAD_DIM)
    n_hg = N_HEADS // ATT_HEADS
    tile = _nbytes((N_CLASSES, CLASS_LEN, ATT_HEADS * HEAD_DIM), F32)
    state = ATT_HEADS * _nbytes((N_CLASSES, CLASS_LEN, HEAD_DIM), F32)
    vmem = _vmem_limit(6 * tile, tile, 3 * state)
    return pl.pallas_call(
        _attn_kernel,
        grid=(BATCH, n_hg),
        in_specs=[
            pl.BlockSpec(blk, lambda b, g: (b, 0, 0, g)),
            pl.BlockSpec(blk, lambda b, g: (b, 0, 0, n_hg + g)),
            pl.BlockSpec(blk, lambda b, g: (b, 0, 0, 2 * n_hg + g)),
        ],
        out_specs=pl.BlockSpec(blk, lambda b, g: (b, 0, 0, g)),
        out_shape=jax.ShapeDtypeStruct((BATCH, N_CLASSES, CLASS_LEN, D_MODEL), BF16),
        scratch_shapes=[
            pltpu.VMEM((ATT_HEADS, N_CLASSES, CLASS_LEN, HEAD_DIM), F32),
            pltpu.VMEM((ATT_HEADS, N_CLASSES, CLASS_LEN, LANES), F32),
            pltpu.VMEM((ATT_HEADS, N_CLASSES, CLASS_LEN, LANES), F32),
        ],
        compiler_params=pltpu.CompilerParams(
            dimension_semantics=("parallel", "parallel"), vmem_limit_bytes=vmem),
        name="attn_core",
    )(qkv, qkv, qkv)


def _attn_out_kernel(a_ref, p_ref, w_ref, x_ref, o_ref, l_ref):
    @pl.when(pl.program_id(2) == 0)
    def _():
        for g in range(ATT_TM // PERM_GROUP):
            ag = a_ref[:, g * N_CLASSES:(g + 1) * N_CLASSES, :].reshape(PERM_GROUP, D_MODEL)
            l_ref[g * PERM_GROUP:(g + 1) * PERM_GROUP, :] = jnp.dot(
                p_ref[...], ag, preferred_element_type=F32).astype(BF16)

    o_ref[...] = x_ref[...] + jnp.dot(l_ref[...], w_ref[...], preferred_element_type=F32)


def _attn_out(att, w_o, x, perm, layer):
    xblk = pl.BlockSpec((None, ATT_TM, AO_TN), lambda b, m, n: (b, m, n))
    vmem = _vmem_limit(
        3 * _nbytes((ATT_TM, D_MODEL), BF16), 2 * _nbytes((D_MODEL, AO_TN), BF16),
        4 * _nbytes((ATT_TM, AO_TN), F32))
    return pl.pallas_call(
        _attn_out_kernel,
        grid=(BATCH, SEQ // ATT_TM, D_MODEL // AO_TN),
        in_specs=[
            pl.BlockSpec((None, N_CLASSES, ATT_JB, D_MODEL), lambda b, m, n: (b, 0, m, 0)),
            pl.BlockSpec(perm.shape, lambda b, m, n: (0, 0)),
            pl.BlockSpec((None, D_MODEL, AO_TN), lambda b, m, n: (layer, 0, n)),
            xblk,
        ],
        out_specs=xblk,
        out_shape=jax.ShapeDtypeStruct(x.shape, F32),
        scratch_shapes=[pltpu.VMEM((ATT_TM, D_MODEL), BF16)],
        compiler_params=pltpu.CompilerParams(
            dimension_semantics=("parallel", "parallel", "arbitrary"), vmem_limit_bytes=vmem),
        name="attn_out",
    )(att, perm, w_o, x)


def kernel(x, mix_norm, mlp_norm, final_norm, mlp_w1, mlp_w2,
           lru_w_in, lru_conv_w, lru_conv_b, lru_w_a, lru_b_a, lru_w_x, lru_b_x,
           lru_lambda, lru_w_out, attn_w_qkv, attn_w_o):
    assert x.shape == (BATCH, SEQ, D_MODEL) and x.dtype == F32
    row = lambda v: v.reshape(1, -1)
    tables = _rope_tables()
    perm_bt = _swap_perm(BATCH, SUBLANES)
    perm_sq = _swap_perm(N_CLASSES, N_CLASSES)
    gf = row(final_norm)
    (mlp_w1, mlp_w2, lru_w_in, lru_w_a, lru_w_x, lru_w_out, attn_w_qkv, attn_w_o) = (
        w.astype(BF16) for w in (mlp_w1, mlp_w2, lru_w_in, lru_w_a, lru_w_x, lru_w_out, attn_w_qkv, attn_w_o))
    for i in range(DEPTH):
        j = i // N_MIXERS
        g = row(mix_norm[i])
        if i % N_MIXERS == 0:
            xg = _lru_in(x, g, lru_w_in, perm_bt, j)
            y = _lru_core(xg, lru_conv_w[j], row(lru_conv_b[j]), lru_w_a, row(lru_b_a[j]),
                          lru_w_x, row(lru_b_x[j]), row(lru_lambda[j]), j)
            x = _lru_out(y, lru_w_out, x, perm_sq, j)
        else:
            qkv = _qkv(x, g, attn_w_qkv, tables, perm_sq, j)
            att = _attn(qkv)
            x = _attn_out(att, attn_w_o, x, perm_sq, j)
        x2d = _mlp(x.reshape(BATCH * SEQ, D_MODEL), row(mlp_norm[i]), mlp_w1, mlp_w2, gf, i == DEPTH - 1, i)
        x = x2d.reshape(BATCH, SEQ, D_MODEL)
    return x
```

```python
import functools
import math

import jax
import jax.numpy as jnp
import numpy as np
from jax import lax
from jax.experimental import pallas as pl
from jax.experimental.pallas import tpu as pltpu

D_MODEL = 2048
BATCH = 16
SEQ = 2048
DEPTH = 4
N_MIXERS = 2
D_RNN = 2560
LRU_BLOCKS = 10
LRU_BLOCK_W = D_RNN // LRU_BLOCKS
CONV_W = 4
LRU_C = 8.0
N_HEADS = 16
HEAD_DIM = D_MODEL // N_HEADS
ROT_DIM = HEAD_DIM // 4
ROPE_THETA = 500000.0
DILATED_PAIRS = ((128, 1), (512, 4), (2048, 16))
SUB_BLOCK = 128
D_FF = 4 * D_MODEL
EPS = 1e-6
NEG_INF = -1e30

V7X_VMEM_BYTES = 64 * 1024 * 1024
V7X_VMEM_RESERVED_BYTES = 4 * 1024 * 1024
MOSAIC_TEMP_BYTES = 10 * 1024 * 1024
SUBLANES = 8
LANES = 128

N_CLASSES = 16
CLASS_LEN = SEQ // N_CLASSES

F32 = jnp.float32
BF16 = jnp.bfloat16
LOG2_E = math.log2(math.e)
GELU_K = math.sqrt(2.0 / math.pi)


def _vmem_limit(*buffer_bytes):
    need = sum(buffer_bytes) + MOSAIC_TEMP_BYTES
    return int(min(need, V7X_VMEM_BYTES - V7X_VMEM_RESERVED_BYTES))


def _nbytes(shape, dtype):
    return int(np.prod(shape)) * jnp.dtype(dtype).itemsize


def _rms_rows(x, g):
    ms = jnp.mean(x * x, axis=-1, keepdims=True)
    return (x * lax.rsqrt(ms + EPS) * g).astype(BF16)


MLP_TM = 1024
MLP_TF = 512
NORM_CHUNK = 128


def _mlp_kernel(x_ref, g_ref, w1_ref, w2_ref, gf_ref, o_ref, h_ref, *, final_norm):
    f = pl.program_id(1)

    @pl.when(f == 0)
    def _():
        def body(c, carry):
            rows = pl.ds(pl.multiple_of(c * NORM_CHUNK, NORM_CHUNK), NORM_CHUNK)
            x = x_ref[rows, :]
            h_ref[rows, :] = _rms_rows(x, g_ref[...])
            o_ref[rows, :] = x
            return carry
        lax.fori_loop(0, MLP_TM // NORM_CHUNK, body, 0)

    u = jnp.dot(h_ref[...], w1_ref[...], preferred_element_type=F32)
    u = jnp.square(jnp.maximum(u, 0.0)).astype(BF16)
    o_ref[...] += jnp.dot(u, w2_ref[...], preferred_element_type=F32)

    if final_norm:
        @pl.when(f == pl.num_programs(1) - 1)
        def _():
            def body(c, carry):
                rows = pl.ds(pl.multiple_of(c * NORM_CHUNK, NORM_CHUNK), NORM_CHUNK)
                y = o_ref[rows, :]
                ms = jnp.mean(y * y, axis=-1, keepdims=True)
                o_ref[rows, :] = y * lax.rsqrt(ms + EPS) * gf_ref[...]
                return carry
            lax.fori_loop(0, MLP_TM // NORM_CHUNK, body, 0)


def _mlp(x2d, g, w1, w2, gf, final_norm, layer):
    rows = x2d.shape[0]
    vmem = _vmem_limit(
        2 * _nbytes((MLP_TM, D_MODEL), F32), 2 * _nbytes((MLP_TM, D_MODEL), F32),
        _nbytes((MLP_TM, D_MODEL), BF16),
        2 * _nbytes((D_MODEL, MLP_TF), BF16), 2 * _nbytes((MLP_TF, D_MODEL), BF16))
    return pl.pallas_call(
        functools.partial(_mlp_kernel, final_norm=final_norm),
        grid=(rows // MLP_TM, D_FF // MLP_TF),
        in_specs=[
            pl.BlockSpec((MLP_TM, D_MODEL), lambda i, f: (i, 0)),
            pl.BlockSpec((1, D_MODEL), lambda i, f: (0, 0)),
            pl.BlockSpec((None, D_MODEL, MLP_TF), lambda i, f: (layer, 0, f)),
            pl.BlockSpec((None, MLP_TF, D_MODEL), lambda i, f: (layer, f, 0)),
            pl.BlockSpec((1, D_MODEL), lambda i, f: (0, 0)),
        ],
        out_specs=pl.BlockSpec((MLP_TM, D_MODEL), lambda i, f: (i, 0)),
        out_shape=jax.ShapeDtypeStruct(x2d.shape, F32),
        scratch_shapes=[pltpu.VMEM((MLP_TM, D_MODEL), BF16)],
        compiler_params=pltpu.CompilerParams(
            dimension_semantics=("parallel", "arbitrary"), vmem_limit_bytes=vmem),
        name="mlp",
    )(x2d, g, w1, w2, gf)


def _swap_perm(n_a, n_b):
    p = np.zeros((n_a * n_b, n_a * n_b), np.float32)
    a, b = np.meshgrid(np.arange(n_a), np.arange(n_b), indexing="ij")
    p[(b * n_a + a).ravel(), (a * n_b + b).ravel()] = 1.0
    return jnp.asarray(p, dtype=BF16)


LRU_TS = 64
LRU_TN = 1024


def _lru_in_kernel(x_ref, g_ref, p_ref, w_ref, o_ref, h_ref):
    @pl.when(pl.program_id(1) == 0)
    def _():
        grp = BATCH * SUBLANES
        for sg in range(LRU_TS // SUBLANES):
            x = x_ref[:, sg * SUBLANES:(sg + 1) * SUBLANES, :].reshape(grp, D_MODEL)
            hp = jnp.dot(p_ref[...], _rms_rows(x, g_ref[...]), preferred_element_type=F32)
            h_ref[sg * grp:(sg + 1) * grp, :] = hp.astype(BF16)

    res = jnp.dot(h_ref[...], w_ref[...], preferred_element_type=F32)
    o_ref[...] = res.reshape(LRU_TS, BATCH, LRU_TN).astype(o_ref.dtype)


def _lru_in(x, g, w_in, perm, layer):
    n_out = 2 * D_RNN
    vmem = _vmem_limit(
        2 * _nbytes((BATCH, LRU_TS, D_MODEL), F32), 2 * _nbytes((D_MODEL, LRU_TN), BF16),
        2 * _nbytes((LRU_TS, BATCH, LRU_TN), BF16), _nbytes((LRU_TS * BATCH, D_MODEL), BF16))
    return pl.pallas_call(
        _lru_in_kernel,
        grid=(SEQ // LRU_TS, n_out // LRU_TN),
        in_specs=[
            pl.BlockSpec((BATCH, LRU_TS, D_MODEL), lambda i, j: (0, i, 0)),
            pl.BlockSpec((1, D_MODEL), lambda i, j: (0, 0)),
            pl.BlockSpec(perm.shape, lambda i, j: (0, 0)),
            pl.BlockSpec((None, D_MODEL, LRU_TN), lambda i, j: (layer, 0, j)),
        ],
        out_specs=pl.BlockSpec((LRU_TS, BATCH, LRU_TN), lambda i, j: (i, 0, j)),
        out_shape=jax.ShapeDtypeStruct((SEQ, BATCH, n_out), BF16),
        scratch_shapes=[pltpu.VMEM((LRU_TS * BATCH, D_MODEL), BF16)],
        compiler_params=pltpu.CompilerParams(
            dimension_semantics=("parallel", "arbitrary"), vmem_limit_bytes=vmem),
        name="lru_in",
    )(x, g, perm, w_in)


LRU_TAIL_TS = 16
LRU_TAIL_TN = 256


def _lru_tail_kernel(xg_ref, cw_ref, cb_ref, wa_ref, ba_ref, wx_ref, bx_ref, lam_ref, p_ref, w_ref, x_ref,
                     o_ref, tail_ref, h_ref, y_ref, l_ref):
    ts, ct = LRU_TAIL_TS, LRU_BLOCK_W
    rows = ts * BATCH
    pad = CONV_W - 1

    @pl.when(pl.program_id(0) == 0)
    def _():
        tail_ref[...] = jnp.zeros(tail_ref.shape, F32)
        h_ref[...] = jnp.zeros(h_ref.shape, F32)
        y_ref[...] = jnp.zeros(y_ref.shape, BF16)

    yp = jnp.dot(p_ref[...], y_ref[...], preferred_element_type=F32).astype(BF16)
    for b in range(BATCH):
        l_ref[b] = yp[b * ts:(b + 1) * ts]
    lhs = l_ref[...].reshape(rows, D_RNN)

    def project(n):
        cols = slice(n * LRU_TAIL_TN, (n + 1) * LRU_TAIL_TN)
        res = jnp.dot(lhs, w_ref[:, cols], preferred_element_type=F32)
        o_ref[:, :, cols] = x_ref[:, :, cols] + res.reshape(BATCH, ts, LRU_TAIL_TN)

    z = -lam_ref[...]
    decay = (-LRU_C) * (jnp.maximum(z, 0.0) + jnp.log1p(jnp.exp(-jnp.abs(z))))
    sigmoid = lambda t: 1.0 / (1.0 + jnp.exp2(t * (-LOG2_E)))

    def conv_gates(cg):
        cols = slice(cg * ct, (cg + 1) * ct)
        xb = xg_ref[:, :, cols].astype(F32)
        ext = jnp.concatenate([tail_ref[:, :, cols], xb], axis=0)
        tail_ref[:, :, cols] = xb[ts - pad:ts]
        xc = cb_ref[:, cols].reshape(1, 1, ct) + sum(
            cw_ref[k:k + 1, cols].reshape(1, 1, ct) * ext[k:k + ts] for k in range(CONV_W))
        xc2 = xc.reshape(rows, ct)
        xcb = xc2.astype(BF16)
        return (xc2, jnp.dot(xcb, wa_ref[cg], preferred_element_type=F32) + ba_ref[:, cols],
                jnp.dot(xcb, wx_ref[cg], preferred_element_type=F32) + bx_ref[:, cols])

    def recur(cg, xc2, pre_r, pre_i):
        for half in range(ct // LANES):
            lo = half * LANES
            cols = slice(cg * ct + lo, cg * ct + lo + LANES)
            r = sigmoid(pre_r[:, lo:lo + LANES])
            gate = sigmoid(pre_i[:, lo:lo + LANES])
            log_a = r * decay[:, cols]
            a = jnp.exp(log_a).reshape(ts, BATCH, LANES)
            th = jnp.tanh(log_a)
            gap = -2.0 * th / (1.0 - th)
            mult = jnp.where(gap > 0.0, gap * lax.rsqrt(gap), 0.0)
            u = (xc2[:, lo:lo + LANES] * gate * mult).reshape(ts, BATCH, LANES)
            h = h_ref[:, cols]
            hs = []
            for s in range(ts):
                h = a[s] * h + u[s]
                hs.append(h)
            h_ref[:, cols] = h
            gb = xg_ref[:, :, D_RNN + cg * ct + lo:D_RNN + cg * ct + lo + LANES].astype(F32)
            inner = gb * (GELU_K + (0.044715 * GELU_K) * (gb * gb))
            y = (0.5 * jnp.stack(hs)) * (gb * (1.0 + jnp.tanh(inner)))
            y_ref[:, cols] = y.reshape(rows, LANES).astype(BF16)

    n_proj = D_MODEL // LRU_TAIL_TN
    staged = conv_gates(0)
    for k in range(max(LRU_BLOCKS, n_proj)):
        if k < n_proj:
            project(k)
        if k < LRU_BLOCKS:
            ready, staged = staged, (conv_gates(k + 1) if k + 1 < LRU_BLOCKS else None)
            recur(k, *ready)


def _lru_tail(xg, conv_w, conv_b, w_a, b_a, w_x, b_x, lam, w_out, x, perm, layer):
    ts = LRU_TAIL_TS
    n_blk = SEQ // ts
    slab = _nbytes((ts, BATCH, D_RNN), F32)
    vmem = _vmem_limit(
        2 * _nbytes((ts, BATCH, 2 * D_RNN), BF16), 4 * _nbytes((LRU_BLOCKS, LRU_BLOCK_W, LRU_BLOCK_W), BF16),
        2 * _nbytes((D_RNN, D_MODEL), BF16), 4 * _nbytes((BATCH, ts, D_MODEL), F32),
        3 * slab, 3 * _nbytes((ts * BATCH, D_RNN), BF16), 4 * slab)
    vec = pl.BlockSpec((1, D_RNN), lambda i: (0, 0))
    gates = pl.BlockSpec((None, LRU_BLOCKS, LRU_BLOCK_W, LRU_BLOCK_W), lambda i: (layer, 0, 0, 0))
    xblk = pl.BlockSpec((BATCH, ts, D_MODEL), lambda i: (0, jnp.maximum(i - 1, 0), 0))
    return pl.pallas_call(
        _lru_tail_kernel,
        grid=(n_blk + 1,),
        in_specs=[
            pl.BlockSpec((ts, BATCH, 2 * D_RNN), lambda i: (jnp.minimum(i, n_blk - 1), 0, 0)),
            pl.BlockSpec((CONV_W, D_RNN), lambda i: (0, 0)),
            vec, gates, vec, gates, vec, vec,
            pl.BlockSpec(perm.shape, lambda i: (0, 0)),
            pl.BlockSpec((None, D_RNN, D_MODEL), lambda i: (layer, 0, 0)),
            xblk,
        ],
        out_specs=xblk,
        out_shape=jax.ShapeDtypeStruct(x.shape, F32),
        scratch_shapes=[
            pltpu.VMEM((CONV_W - 1, BATCH, D_RNN), F32),
            pltpu.VMEM((BATCH, D_RNN), F32),
            pltpu.VMEM((ts * BATCH, D_RNN), BF16),
            pltpu.VMEM((BATCH, ts, D_RNN), BF16),
        ],
        compiler_params=pltpu.CompilerParams(dimension_semantics=("arbitrary",), vmem_limit_bytes=vmem),
        name="lru_tail",
    )(xg, conv_w, conv_b, w_a, b_a, w_x, b_x, lam, perm, w_out, x)


ATT_TM = 1024
ATT_JB = ATT_TM // N_CLASSES
PERM_GROUP = N_CLASSES * N_CLASSES
QKV_TN = 1024
QKV_CHUNK = 256
ATT_HEADS = 4
AO_TN = 1024
QCHUNKS = 4
QCHUNK = SUB_BLOCK // QCHUNKS
SCORE_SCALE = HEAD_DIM ** -0.5 * math.log2(math.e)


def _rope_tables():
    inv = ROPE_THETA ** (-jnp.arange(0, ROT_DIM, 2, dtype=F32) / ROT_DIM)
    pos = (jnp.arange(CLASS_LEN, dtype=F32)[None, :] * N_CLASSES
           + jnp.arange(N_CLASSES, dtype=F32)[:, None])
    ang = pos[..., None] * inv
    cos, sin = jnp.cos(ang), jnp.sin(ang)
    ones = jnp.ones(pos.shape + (HEAD_DIM - ROT_DIM,), F32)
    zeros_h = jnp.zeros_like(cos)
    zeros_t = jnp.zeros_like(ones)
    c = jnp.concatenate([cos, cos, ones], axis=-1)
    s_up = jnp.concatenate([-sin, zeros_h, zeros_t], axis=-1)
    s_dn = jnp.concatenate([zeros_h, sin, zeros_t], axis=-1)
    ident = (jnp.ones_like(c), jnp.zeros_like(c), jnp.zeros_like(c))
    return tuple(jnp.stack([t * SCORE_SCALE, t, e]) for t, e in zip((c, s_up, s_dn), ident))


def _qkv_kernel(x_ref, g_ref, p_ref, w_ref, c_ref, su_ref, sd_ref, o_ref, h_ref):
    n = pl.program_id(2)

    @pl.when(n == 0)
    def _():
        for g in range(ATT_TM // PERM_GROUP):
            x = x_ref[g * PERM_GROUP:(g + 1) * PERM_GROUP, :]
            hp = jnp.dot(p_ref[...], _rms_rows(x, g_ref[...]), preferred_element_type=F32).astype(BF16)
            for r in range(N_CLASSES):
                h_ref[r, g * N_CLASSES:(g + 1) * N_CLASSES, :] = hp[r * N_CLASSES:(r + 1) * N_CLASSES]

    hmat = h_ref[...].reshape(ATT_TM, D_MODEL)
    c = c_ref[...].reshape(ATT_TM, HEAD_DIM)
    su = su_ref[...].reshape(ATT_TM, HEAD_DIM)
    sd = sd_ref[...].reshape(ATT_TM, HEAD_DIM)
    for ch in range(QKV_TN // QKV_CHUNK):
        res = jnp.dot(hmat, w_ref[:, ch * QKV_CHUNK:(ch + 1) * QKV_CHUNK], preferred_element_type=F32)
        for hh in range(QKV_CHUNK // HEAD_DIM):
            t = res[:, hh * HEAD_DIM:(hh + 1) * HEAD_DIM]
            up = pltpu.roll(t, HEAD_DIM - ROT_DIM // 2, axis=1)
            dn = pltpu.roll(t, ROT_DIM // 2, axis=1)
            col = ch * QKV_CHUNK + hh * HEAD_DIM
            o_ref[:, :, col:col + HEAD_DIM] = (t * c + up * su + dn * sd).reshape(N_CLASSES, ATT_JB, HEAD_DIM)


def _qkv(x, g, w_qkv, tables, perm, layer):
    n_out = 3 * D_MODEL
    q_tiles = D_MODEL // QKV_TN
    tab = pl.BlockSpec((None, N_CLASSES, ATT_JB, HEAD_DIM), lambda b, m, n: (n // q_tiles, 0, m, 0))
    vmem = _vmem_limit(
        2 * _nbytes((ATT_TM, D_MODEL), F32), 2 * _nbytes((D_MODEL, QKV_TN), BF16),
        2 * _nbytes((ATT_TM, QKV_TN), F32), _nbytes((ATT_TM, D_MODEL), BF16),
        6 * _nbytes((ATT_TM, HEAD_DIM), F32))
    return pl.pallas_call(
        _qkv_kernel,
        grid=(BATCH, SEQ // ATT_TM, n_out // QKV_TN),
        in_specs=[
            pl.BlockSpec((None, ATT_TM, D_MODEL), lambda b, m, n: (b, m, 0)),
            pl.BlockSpec((1, D_MODEL), lambda b, m, n: (0, 0)),
            pl.BlockSpec(perm.shape, lambda b, m, n: (0, 0)),
            pl.BlockSpec((None, D_MODEL, QKV_TN), lambda b, m, n: (layer, 0, n)),
            tab, tab, tab,
        ],
        out_specs=pl.BlockSpec((None, N_CLASSES, ATT_JB, QKV_TN), lambda b, m, n: (b, 0, m, n)),
        out_shape=jax.ShapeDtypeStruct((BATCH, N_CLASSES, CLASS_LEN, n_out), F32),
        scratch_shapes=[pltpu.VMEM((N_CLASSES, ATT_JB, D_MODEL), BF16)],
        compiler_params=pltpu.CompilerParams(
            dimension_semantics=("parallel", "parallel", "arbitrary"), vmem_limit_bytes=vmem),
        name="attn_qkv",
    )(x, g, perm, w_qkv, *tables)


def _tile_order(kind):
    a = lax.broadcasted_iota(jnp.int32, (SUB_BLOCK, SUB_BLOCK), 0)
    b = lax.broadcasted_iota(jnp.int32, (SUB_BLOCK, SUB_BLOCK), 1)
    if kind == 1:
        f = lambda v: N_CLASSES * (v % SUBLANES) + v // SUBLANES
    else:
        f = lambda v: QCHUNKS * (v % QCHUNK) + v // QCHUNK
    return f(a), f(b)


def _softmax_tiles(tiles):
    ss = [[jnp.where(mk, _dot_nt(q, k), NEG_INF) for k, mk in zip(ks, masks)] for q, ks, _, masks in tiles]
    ms, ps, ls = _softmax_rows(ss)
    accs = [functools.reduce(jnp.add, [jnp.dot(pb, v, preferred_element_type=F32) for pb, v in zip(p, vs)])
            for p, (_, _, vs, _) in zip(ps, tiles)]
    return list(zip(ms, ls, accs))


def _dot_nt(a, b):
    return lax.dot_general(a, b, (((1,), (1,)), ((), ())), preferred_element_type=F32)


def _softmax_rows(ss):
    full = (SUB_BLOCK, SUB_BLOCK)
    ms = [jnp.broadcast_to(functools.reduce(jnp.maximum, s).max(axis=-1, keepdims=True), full) for s in ss]
    ps = [[jnp.exp2(sb - m) for sb in s] for s, m in zip(ss, ms)]
    ls = [jnp.broadcast_to(functools.reduce(jnp.add, p).sum(axis=-1, keepdims=True), full) for p in ps]
    return ms, [[pb.astype(BF16) for pb in p] for p in ps], ls


def _first_pass(groups, cur4, prev4, causal):
    n_blk = CLASS_LEN // QCHUNK
    rows = lambda t, n: t[n * QCHUNK:(n + 1) * QCHUNK]
    regroup = lambda blocks, n: jnp.concatenate([rows(b, n) for b in blocks], axis=0)
    blocks = range(n_blk)
    qt = [[regroup(q_c, n) for n in blocks] for q_c, _, _ in groups]
    kt = [[regroup(k_c, n) for n in blocks] for _, k_c, _ in groups]
    vt = [[regroup(v_c, n) for n in blocks] for _, _, v_c in groups]
    s16 = [[jnp.where(causal, _dot_nt(q, k), NEG_INF) for q, k in zip(q_c, k_c)] for q_c, k_c, _ in groups]
    ss = []
    for g in range(len(groups)):
        for n in blocks:
            s = [jnp.where(cur4, _dot_nt(qt[g][n], kt[g][n]), NEG_INF), regroup(s16[g], n)]
            if n > 0:
                s.append(jnp.where(prev4, _dot_nt(qt[g][n], kt[g][n - 1]), NEG_INF))
            ss.append(s)
    ms, ps, ls = _softmax_rows(ss)
    out = []
    for g, (_, _, v_c) in enumerate(groups):
        tile = lambda n: g * n_blk + n
        pv16 = [jnp.dot(regroup([ps[tile(n)][1] for n in blocks], c), v_c[c], preferred_element_type=F32)
                for c in range(QCHUNKS)]
        accs = []
        for n in blocks:
            acc = jnp.dot(ps[tile(n)][0], vt[g][n], preferred_element_type=F32) + regroup(pv16, n)
            if n > 0:
                acc = acc + jnp.dot(ps[tile(n)][2], vt[g][n - 1], preferred_element_type=F32)
            accs.append(acc)
        out.append([(regroup([ms[tile(n)] for n in blocks], c), regroup([ls[tile(n)] for n in blocks], c),
                     regroup(accs, c)) for c in range(QCHUNKS)])
    return out


def _merge_all(olds, news):
    ms = [jnp.maximum(o[0], n[0]) for o, n in zip(olds, news)]
    a0 = [jnp.exp2(o[0] - m) for o, m in zip(olds, ms)]
    a1 = [jnp.exp2(n[0] - m) for n, m in zip(news, ms)]
    ls = [x * o[1] + y * n[1] for x, y, o, n in zip(a0, a1, olds, news)]
    accs = [x * o[2] + y * n[2] for x, y, o, n in zip(a0, a1, olds, news)]
    return list(zip(ms, ls, accs))


def _attn_kernel(q_ref, k_ref, v_ref, o_ref, acc_ref, m_ref, l_ref):
    ua1, ub1 = _tile_order(1)
    ua4, ub4 = _tile_order(4)
    cur1, cur4, prev4 = ub1 <= ua1, ub4 <= ua4, ub4 >= ua4
    causal = (lax.broadcasted_iota(jnp.int32, (SUB_BLOCK, SUB_BLOCK), 1)
              <= lax.broadcasted_iota(jnp.int32, (SUB_BLOCK, SUB_BLOCK), 0))
    heads = [slice(hh * HEAD_DIM, (hh + 1) * HEAD_DIM) for hh in range(ATT_HEADS)]
    bf = lambda t: t.astype(BF16)

    def first(r4, carry):
        classes = [r4 + QCHUNKS * c for c in range(QCHUNKS)]
        groups = [tuple([bf(ref[r, :, cs]) for r in classes] for ref in (q_ref, k_ref, v_ref)) for cs in heads]
        for hh, per_class in enumerate(_first_pass(groups, cur4, prev4, causal)):
            for r, (m, l, acc) in zip(classes, per_class):
                m_ref[hh, r] = m
                l_ref[hh, r] = l
                acc_ref[hh, r] = acc
        return carry
    lax.fori_loop(0, QCHUNKS, first, 0)

    rows = N_CLASSES * SUBLANES

    def d1(n2, carry):
        start = pl.multiple_of(n2 * 2 * SUBLANES, 2 * SUBLANES)
        prev = pl.multiple_of(jnp.maximum(n2 * 2 - 1, 0) * SUBLANES, SUBLANES)
        prev1 = ub1 >= ua1
        prev1_first = ub1 >= ua1 + jnp.where(n2 > 0, 0, SUB_BLOCK)
        pair = pl.ds(start, 2 * SUBLANES)
        lo, hi = slice(0, SUBLANES), slice(SUBLANES, 2 * SUBLANES)
        flat = lambda t: t.reshape(rows, t.shape[-1])
        tiles, olds = [], []
        for hh, cs in enumerate(heads):
            qb, kb, vb = q_ref[:, pair, cs], k_ref[:, pair, cs], v_ref[:, pair, cs]
            kp, vp = k_ref[:, pl.ds(prev, SUBLANES), cs], v_ref[:, pl.ds(prev, SUBLANES), cs]
            mb, lb, ab = m_ref[hh, :, pair, :], l_ref[hh, :, pair, :], acc_ref[hh, :, pair, :]
            for half, kprev, vprev, pmask in ((lo, kp, vp, prev1_first), (hi, kb[:, lo], vb[:, lo], prev1)):
                tiles.append((bf(flat(qb[:, half])),
                              [bf(flat(kprev)), bf(flat(kb[:, half]))],
                              [bf(flat(vprev)), bf(flat(vb[:, half]))], [pmask, cur1]))
                olds.append((flat(mb[:, half]), flat(lb[:, half]), flat(ab[:, half])))
        outs = [(acc / l).reshape(N_CLASSES, SUBLANES, HEAD_DIM)
                for _, l, acc in _merge_all(olds, _softmax_tiles(tiles))]
        for hh, cs in enumerate(heads):
            o_ref[:, pair, cs] = jnp.concatenate(outs[2 * hh:2 * hh + 2], axis=1).astype(o_ref.dtype)
        return carry
    lax.fori_loop(0, CLASS_LEN // (2 * SUBLANES), d1, 0)


def _attn(qkv):
    blk = (None, N_CLASSES, CLASS_LEN, ATT_HEADS * HEAD_DIM)
    n_hg = N_HEADS // ATT_HEADS
    tile = _nbytes((N_CLASSES, CLASS_LEN, ATT_HEADS * HEAD_DIM), F32)
    state = ATT_HEADS * _nbytes((N_CLASSES, CLASS_LEN, HEAD_DIM), F32)
    vmem = _vmem_limit(6 * tile, tile, 3 * state)
    return pl.pallas_call(
        _attn_kernel,
        grid=(BATCH, n_hg),
        in_specs=[
            pl.BlockSpec(blk, lambda b, g: (b, 0, 0, g)),
            pl.BlockSpec(blk, lambda b, g: (b, 0, 0, n_hg + g)),
            pl.BlockSpec(blk, lambda b, g: (b, 0, 0, 2 * n_hg + g)),
        ],
        out_specs=pl.BlockSpec(blk, lambda b, g: (b, 0, 0, g)),
        out_shape=jax.ShapeDtypeStruct((BATCH, N_CLASSES, CLASS_LEN, D_MODEL), BF16),
        scratch_shapes=[
            pltpu.VMEM((ATT_HEADS, N_CLASSES, CLASS_LEN, HEAD_DIM), F32),
            pltpu.VMEM((ATT_HEADS, N_CLASSES, CLASS_LEN, LANES), F32),
            pltpu.VMEM((ATT_HEADS, N_CLASSES, CLASS_LEN, LANES), F32),
        ],
        compiler_params=pltpu.CompilerParams(
            dimension_semantics=("parallel", "parallel"), vmem_limit_bytes=vmem),
        name="attn_core",
    )(qkv, qkv, qkv)


def _attn_out_kernel(a_ref, p_ref, w_ref, x_ref, o_ref, l_ref):
    @pl.when(pl.program_id(2) == 0)
    def _():
        for g in range(ATT_TM // PERM_GROUP):
            ag = a_ref[:, g * N_CLASSES:(g + 1) * N_CLASSES, :].reshape(PERM_GROUP, D_MODEL)
            l_ref[g * PERM_GROUP:(g + 1) * PERM_GROUP, :] = jnp.dot(
                p_ref[...], ag, preferred_element_type=F32).astype(BF16)

    o_ref[...] = x_ref[...] + jnp.dot(l_ref[...], w_ref[...], preferred_element_type=F32)


def _attn_out(att, w_o, x, perm, layer):
    xblk = pl.BlockSpec((None, ATT_TM, AO_TN), lambda b, m, n: (b, m, n))
    vmem = _vmem_limit(
        3 * _nbytes((ATT_TM, D_MODEL), BF16), 2 * _nbytes((D_MODEL, AO_TN), BF16),
        4 * _nbytes((ATT_TM, AO_TN), F32))
    return pl.pallas_call(
        _attn_out_kernel,
        grid=(BATCH, SEQ // ATT_TM, D_MODEL // AO_TN),
        in_specs=[
            pl.BlockSpec((None, N_CLASSES, ATT_JB, D_MODEL), lambda b, m, n: (b, 0, m, 0)),
            pl.BlockSpec(perm.shape, lambda b, m, n: (0, 0)),
            pl.BlockSpec((None, D_MODEL, AO_TN), lambda b, m, n: (layer, 0, n)),
            xblk,
        ],
        out_specs=xblk,
        out_shape=jax.ShapeDtypeStruct(x.shape, F32),
        scratch_shapes=[pltpu.VMEM((ATT_TM, D_MODEL), BF16)],
        compiler_params=pltpu.CompilerParams(
            dimension_semantics=("parallel", "parallel", "arbitrary"), vmem_limit_bytes=vmem),
        name="attn_out",
    )(att, perm, w_o, x)


def kernel(x, mix_norm, mlp_norm, final_norm, mlp_w1, mlp_w2,
           lru_w_in, lru_conv_w, lru_conv_b, lru_w_a, lru_b_a, lru_w_x, lru_b_x,
           lru_lambda, lru_w_out, attn_w_qkv, attn_w_o):
    assert x.shape == (BATCH, SEQ, D_MODEL) and x.dtype == F32
    row = lambda v: v.reshape(1, -1)
    tables = _rope_tables()
    perm_bt = _swap_perm(BATCH, SUBLANES)
    perm_sq = _swap_perm(N_CLASSES, N_CLASSES)
    gf = row(final_norm)
    (mlp_w1, mlp_w2, lru_w_in, lru_w_a, lru_w_x, lru_w_out, attn_w_qkv, attn_w_o) = (
        w.astype(BF16) for w in (mlp_w1, mlp_w2, lru_w_in, lru_w_a, lru_w_x, lru_w_out, attn_w_qkv, attn_w_o))
    for i in range(DEPTH):
        j = i // N_MIXERS
        g = row(mix_norm[i])
        if i % N_MIXERS == 0:
            xg = _lru_in(x, g, lru_w_in, perm_bt, j)
            x = _lru_tail(xg, lru_conv_w[j], row(lru_conv_b[j]), lru_w_a, row(lru_b_a[j]),
                          lru_w_x, row(lru_b_x[j]), row(lru_lambda[j]), lru_w_out, x, perm_sq, j)
        else:
            qkv = _qkv(x, g, attn_w_qkv, tables, perm_sq, j)
            att = _attn(qkv)
            x = _attn_out(att, attn_w_o, x, perm_sq, j)
        x2d = _mlp(x.reshape(BATCH * SEQ, D_MODEL), row(mlp_norm[i]), mlp_w1, mlp_w2, gf, i == DEPTH - 1, i)
        x = x2d.reshape(BATCH, SEQ, D_MODEL)
    return x
```

```python
import functools
import math

import jax
import jax.numpy as jnp
import numpy as np
from jax import lax
from jax.experimental import pallas as pl
from jax.experimental.pallas import tpu as pltpu

D_MODEL = 2048
BATCH = 16
SEQ = 2048
DEPTH = 4
N_MIXERS = 2
D_RNN = 2560
LRU_BLOCKS = 10
LRU_BLOCK_W = D_RNN // LRU_BLOCKS
CONV_W = 4
LRU_C = 8.0
N_HEADS = 16
HEAD_DIM = D_MODEL // N_HEADS
ROT_DIM = HEAD_DIM // 4
ROPE_THETA = 500000.0
DILATED_PAIRS = ((128, 1), (512, 4), (2048, 16))
SUB_BLOCK = 128
D_FF = 4 * D_MODEL
EPS = 1e-6
NEG_INF = -1e30

V7X_VMEM_BYTES = 64 * 1024 * 1024
V7X_VMEM_RESERVED_BYTES = 4 * 1024 * 1024
MOSAIC_TEMP_BYTES = 10 * 1024 * 1024
SUBLANES = 8
LANES = 128

N_CLASSES = 16
CLASS_LEN = SEQ // N_CLASSES

F32 = jnp.float32
BF16 = jnp.bfloat16
LOG2_E = math.log2(math.e)
GELU_K = math.sqrt(2.0 / math.pi)


def _vmem_limit(*buffer_bytes):
    need = sum(buffer_bytes) + MOSAIC_TEMP_BYTES
    return int(min(need, V7X_VMEM_BYTES - V7X_VMEM_RESERVED_BYTES))


def _nbytes(shape, dtype):
    return int(np.prod(shape)) * jnp.dtype(dtype).itemsize


def _rms_rows(x, g):
    ms = jnp.mean(x * x, axis=-1, keepdims=True)
    return (x * lax.rsqrt(ms + EPS) * g).astype(BF16)


MLP_TM = 1024
MLP_TF = 1024
NORM_CHUNK = 128


def _mlp_kernel(x_ref, g_ref, w1_ref, w2_ref, gf_ref, o_ref, h_ref, *, final_norm):
    f = pl.program_id(1)

    @pl.when(f == 0)
    def _():
        def body(c, carry):
            rows = pl.ds(pl.multiple_of(c * NORM_CHUNK, NORM_CHUNK), NORM_CHUNK)
            x = x_ref[rows, :]
            h_ref[rows, :] = _rms_rows(x, g_ref[...])
            o_ref[rows, :] = x
            return carry
        lax.fori_loop(0, MLP_TM // NORM_CHUNK, body, 0)

    u = jnp.dot(h_ref[...], w1_ref[...], preferred_element_type=F32)
    u = jnp.square(jnp.maximum(u, 0.0)).astype(BF16)
    o_ref[...] += jnp.dot(u, w2_ref[...], preferred_element_type=F32)

    if final_norm:
        @pl.when(f == pl.num_programs(1) - 1)
        def _():
            def body(c, carry):
                rows = pl.ds(pl.multiple_of(c * NORM_CHUNK, NORM_CHUNK), NORM_CHUNK)
                y = o_ref[rows, :]
                ms = jnp.mean(y * y, axis=-1, keepdims=True)
                o_ref[rows, :] = y * lax.rsqrt(ms + EPS) * gf_ref[...]
                return carry
            lax.fori_loop(0, MLP_TM // NORM_CHUNK, body, 0)


def _mlp(x2d, g, w1, w2, gf, final_norm, layer):
    rows = x2d.shape[0]
    vmem = _vmem_limit(
        2 * _nbytes((MLP_TM, D_MODEL), F32), 2 * _nbytes((MLP_TM, D_MODEL), F32),
        _nbytes((MLP_TM, D_MODEL), BF16),
        2 * _nbytes((D_MODEL, MLP_TF), BF16), 2 * _nbytes((MLP_TF, D_MODEL), BF16))
    return pl.pallas_call(
        functools.partial(_mlp_kernel, final_norm=final_norm),
        grid=(rows // MLP_TM, D_FF // MLP_TF),
        in_specs=[
            pl.BlockSpec((MLP_TM, D_MODEL), lambda i, f: (i, 0)),
            pl.BlockSpec((1, D_MODEL), lambda i, f: (0, 0)),
            pl.BlockSpec((None, D_MODEL, MLP_TF), lambda i, f: (layer, 0, f)),
            pl.BlockSpec((None, MLP_TF, D_MODEL), lambda i, f: (layer, f, 0)),
            pl.BlockSpec((1, D_MODEL), lambda i, f: (0, 0)),
        ],
        out_specs=pl.BlockSpec((MLP_TM, D_MODEL), lambda i, f: (i, 0)),
        out_shape=jax.ShapeDtypeStruct(x2d.shape, F32),
        scratch_shapes=[pltpu.VMEM((MLP_TM, D_MODEL), BF16)],
        compiler_params=pltpu.CompilerParams(
            dimension_semantics=("parallel", "arbitrary"), vmem_limit_bytes=vmem),
        name="mlp",
    )(x2d, g, w1, w2, gf)


def _swap_perm(n_a, n_b):
    p = np.zeros((n_a * n_b, n_a * n_b), np.float32)
    a, b = np.meshgrid(np.arange(n_a), np.arange(n_b), indexing="ij")
    p[(b * n_a + a).ravel(), (a * n_b + b).ravel()] = 1.0
    return jnp.asarray(p, dtype=BF16)


LRU_TS = 64
LRU_TN = 1024


def _lru_in_kernel(x_ref, g_ref, p_ref, w_ref, o_ref, h_ref):
    @pl.when(pl.program_id(1) == 0)
    def _():
        grp = BATCH * SUBLANES
        for sg in range(LRU_TS // SUBLANES):
            x = x_ref[:, sg * SUBLANES:(sg + 1) * SUBLANES, :].reshape(grp, D_MODEL)
            hp = jnp.dot(p_ref[...], _rms_rows(x, g_ref[...]), preferred_element_type=F32)
            h_ref[sg * grp:(sg + 1) * grp, :] = hp.astype(BF16)

    res = jnp.dot(h_ref[...], w_ref[...], preferred_element_type=F32)
    o_ref[...] = res.reshape(LRU_TS, BATCH, LRU_TN).astype(o_ref.dtype)


def _lru_in(x, g, w_in, perm, layer):
    n_out = 2 * D_RNN
    vmem = _vmem_limit(
        2 * _nbytes((BATCH, LRU_TS, D_MODEL), F32), 2 * _nbytes((D_MODEL, LRU_TN), BF16),
        2 * _nbytes((LRU_TS, BATCH, LRU_TN), BF16), _nbytes((LRU_TS * BATCH, D_MODEL), BF16))
    return pl.pallas_call(
        _lru_in_kernel,
        grid=(SEQ // LRU_TS, n_out // LRU_TN),
        in_specs=[
            pl.BlockSpec((BATCH, LRU_TS, D_MODEL), lambda i, j: (0, i, 0)),
            pl.BlockSpec((1, D_MODEL), lambda i, j: (0, 0)),
            pl.BlockSpec(perm.shape, lambda i, j: (0, 0)),
            pl.BlockSpec((None, D_MODEL, LRU_TN), lambda i, j: (layer, 0, j)),
        ],
        out_specs=pl.BlockSpec((LRU_TS, BATCH, LRU_TN), lambda i, j: (i, 0, j)),
        out_shape=jax.ShapeDtypeStruct((SEQ, BATCH, n_out), BF16),
        scratch_shapes=[pltpu.VMEM((LRU_TS * BATCH, D_MODEL), BF16)],
        compiler_params=pltpu.CompilerParams(
            dimension_semantics=("parallel", "arbitrary"), vmem_limit_bytes=vmem),
        name="lru_in",
    )(x, g, perm, w_in)


LRU_TAIL_TS = 16
LRU_TAIL_TN = 256


def _lru_tail_kernel(xg_ref, cw_ref, cb_ref, wa_ref, ba_ref, wx_ref, bx_ref, lam_ref, p_ref, w_ref, x_ref,
                     o_ref, tail_ref, h_ref, y_ref, l_ref):
    ts, ct = LRU_TAIL_TS, LRU_BLOCK_W
    rows = ts * BATCH
    pad = CONV_W - 1

    @pl.when(pl.program_id(0) == 0)
    def _():
        tail_ref[...] = jnp.zeros(tail_ref.shape, F32)
        h_ref[...] = jnp.zeros(h_ref.shape, F32)
        y_ref[...] = jnp.zeros(y_ref.shape, BF16)

    yp = jnp.dot(p_ref[...], y_ref[...], preferred_element_type=F32).astype(BF16)
    for b in range(BATCH):
        l_ref[b] = yp[b * ts:(b + 1) * ts]
    lhs = l_ref[...].reshape(rows, D_RNN)

    def project(n):
        cols = slice(n * LRU_TAIL_TN, (n + 1) * LRU_TAIL_TN)
        res = jnp.dot(lhs, w_ref[:, cols], preferred_element_type=F32)
        o_ref[:, :, cols] = x_ref[:, :, cols] + res.reshape(BATCH, ts, LRU_TAIL_TN)

    z = -lam_ref[...]
    decay = (-LRU_C) * (jnp.maximum(z, 0.0) + jnp.log1p(jnp.exp(-jnp.abs(z))))
    sigmoid = lambda t: 1.0 / (1.0 + jnp.exp2(t * (-LOG2_E)))

    def conv_gates(cg):
        cols = slice(cg * ct, (cg + 1) * ct)
        xb = xg_ref[:, :, cols].astype(F32)
        ext = jnp.concatenate([tail_ref[:, :, cols], xb], axis=0)
        tail_ref[:, :, cols] = xb[ts - pad:ts]
        xc = cb_ref[:, cols].reshape(1, 1, ct) + sum(
            cw_ref[k:k + 1, cols].reshape(1, 1, ct) * ext[k:k + ts] for k in range(CONV_W))
        xc2 = xc.reshape(rows, ct)
        xcb = xc2.astype(BF16)
        return (xc2, jnp.dot(xcb, wa_ref[cg], preferred_element_type=F32) + ba_ref[:, cols],
                jnp.dot(xcb, wx_ref[cg], preferred_element_type=F32) + bx_ref[:, cols])

    def recur(cg, xc2, pre_r, pre_i):
        for half in range(ct // LANES):
            lo = half * LANES
            cols = slice(cg * ct + lo, cg * ct + lo + LANES)
            r = sigmoid(pre_r[:, lo:lo + LANES])
            gate = sigmoid(pre_i[:, lo:lo + LANES])
            log_a = r * decay[:, cols]
            a = jnp.exp(log_a).reshape(ts, BATCH, LANES)
            th = jnp.tanh(log_a)
            gap = -2.0 * th / (1.0 - th)
            mult = jnp.where(gap > 0.0, gap * lax.rsqrt(gap), 0.0)
            u = (xc2[:, lo:lo + LANES] * gate * mult).reshape(ts, BATCH, LANES)
            h = h_ref[:, cols]
            hs = []
            for s in range(ts):
                h = a[s] * h + u[s]
                hs.append(h)
            h_ref[:, cols] = h
            gb = xg_ref[:, :, D_RNN + cg * ct + lo:D_RNN + cg * ct + lo + LANES].astype(F32)
            inner = gb * (GELU_K + (0.044715 * GELU_K) * (gb * gb))
            y = (0.5 * jnp.stack(hs)) * (gb * (1.0 + jnp.tanh(inner)))
            y_ref[:, cols] = y.reshape(rows, LANES).astype(BF16)

    n_proj = D_MODEL // LRU_TAIL_TN
    staged = conv_gates(0)
    for k in range(max(LRU_BLOCKS, n_proj)):
        if k < n_proj:
            project(k)
        if k < LRU_BLOCKS:
            ready, staged = staged, (conv_gates(k + 1) if k + 1 < LRU_BLOCKS else None)
            recur(k, *ready)


def _lru_tail(xg, conv_w, conv_b, w_a, b_a, w_x, b_x, lam, w_out, x, perm, layer):
    ts = LRU_TAIL_TS
    n_blk = SEQ // ts
    slab = _nbytes((ts, BATCH, D_RNN), F32)
    vmem = _vmem_limit(
        2 * _nbytes((ts, BATCH, 2 * D_RNN), BF16), 4 * _nbytes((LRU_BLOCKS, LRU_BLOCK_W, LRU_BLOCK_W), BF16),
        2 * _nbytes((D_RNN, D_MODEL), BF16), 4 * _nbytes((BATCH, ts, D_MODEL), F32),
        3 * slab, 3 * _nbytes((ts * BATCH, D_RNN), BF16), 4 * slab)
    vec = pl.BlockSpec((1, D_RNN), lambda i: (0, 0))
    gates = pl.BlockSpec((None, LRU_BLOCKS, LRU_BLOCK_W, LRU_BLOCK_W), lambda i: (layer, 0, 0, 0))
    xblk = pl.BlockSpec((BATCH, ts, D_MODEL), lambda i: (0, jnp.maximum(i - 1, 0), 0))
    return pl.pallas_call(
        _lru_tail_kernel,
        grid=(n_blk + 1,),
        in_specs=[
            pl.BlockSpec((ts, BATCH, 2 * D_RNN), lambda i: (jnp.minimum(i, n_blk - 1), 0, 0)),
            pl.BlockSpec((CONV_W, D_RNN), lambda i: (0, 0)),
            vec, gates, vec, gates, vec, vec,
            pl.BlockSpec(perm.shape, lambda i: (0, 0)),
            pl.BlockSpec((None, D_RNN, D_MODEL), lambda i: (layer, 0, 0)),
            xblk,
        ],
        out_specs=xblk,
        out_shape=jax.ShapeDtypeStruct(x.shape, F32),
        scratch_shapes=[
            pltpu.VMEM((CONV_W - 1, BATCH, D_RNN), F32),
            pltpu.VMEM((BATCH, D_RNN), F32),
            pltpu.VMEM((ts * BATCH, D_RNN), BF16),
            pltpu.VMEM((BATCH, ts, D_RNN), BF16),
        ],
        compiler_params=pltpu.CompilerParams(dimension_semantics=("arbitrary",), vmem_limit_bytes=vmem),
        name="lru_tail",
    )(xg, conv_w, conv_b, w_a, b_a, w_x, b_x, lam, perm, w_out, x)


ATT_TM = 1024
ATT_JB = ATT_TM // N_CLASSES
PERM_GROUP = N_CLASSES * N_CLASSES
QKV_TN = 1024
QKV_CHUNK = 256
ATT_HEADS = 4
AO_TN = 1024
QCHUNKS = 4
QCHUNK = SUB_BLOCK // QCHUNKS
SCORE_SCALE = HEAD_DIM ** -0.5 * math.log2(math.e)


def _rope_tables():
    inv = ROPE_THETA ** (-jnp.arange(0, ROT_DIM, 2, dtype=F32) / ROT_DIM)
    pos = (jnp.arange(CLASS_LEN, dtype=F32)[None, :] * N_CLASSES
           + jnp.arange(N_CLASSES, dtype=F32)[:, None])
    ang = pos[..., None] * inv
    cos, sin = jnp.cos(ang), jnp.sin(ang)
    ones = jnp.ones(pos.shape + (HEAD_DIM - ROT_DIM,), F32)
    zeros_h = jnp.zeros_like(cos)
    zeros_t = jnp.zeros_like(ones)
    c = jnp.concatenate([cos, cos, ones], axis=-1)
    s_up = jnp.concatenate([-sin, zeros_h, zeros_t], axis=-1)
    s_dn = jnp.concatenate([zeros_h, sin, zeros_t], axis=-1)
    ident = (jnp.ones_like(c), jnp.zeros_like(c), jnp.zeros_like(c))
    return tuple(jnp.stack([t * SCORE_SCALE, t, e]) for t, e in zip((c, s_up, s_dn), ident))


def _qkv_kernel(x_ref, g_ref, p_ref, w_ref, c_ref, su_ref, sd_ref, o_ref, h_ref):
    n = pl.program_id(2)

    @pl.when(n == 0)
    def _():
        for g in range(ATT_TM // PERM_GROUP):
            x = x_ref[g * PERM_GROUP:(g + 1) * PERM_GROUP, :]
            hp = jnp.dot(p_ref[...], _rms_rows(x, g_ref[...]), preferred_element_type=F32).astype(BF16)
            for r in range(N_CLASSES):
                h_ref[r, g * N_CLASSES:(g + 1) * N_CLASSES, :] = hp[r * N_CLASSES:(r + 1) * N_CLASSES]

    hmat = h_ref[...].reshape(ATT_TM, D_MODEL)
    c = c_ref[...].reshape(ATT_TM, HEAD_DIM)
    su = su_ref[...].reshape(ATT_TM, HEAD_DIM)
    sd = sd_ref[...].reshape(ATT_TM, HEAD_DIM)
    for ch in range(QKV_TN // QKV_CHUNK):
        res = jnp.dot(hmat, w_ref[:, ch * QKV_CHUNK:(ch + 1) * QKV_CHUNK], preferred_element_type=F32)
        for hh in range(QKV_CHUNK // HEAD_DIM):
            t = res[:, hh * HEAD_DIM:(hh + 1) * HEAD_DIM]
            up = pltpu.roll(t, HEAD_DIM - ROT_DIM // 2, axis=1)
            dn = pltpu.roll(t, ROT_DIM // 2, axis=1)
            col = ch * QKV_CHUNK + hh * HEAD_DIM
            o_ref[:, :, col:col + HEAD_DIM] = (t * c + up * su + dn * sd).reshape(N_CLASSES, ATT_JB, HEAD_DIM)


def _qkv(x, g, w_qkv, tables, perm, layer):
    n_out = 3 * D_MODEL
    q_tiles = D_MODEL // QKV_TN
    tab = pl.BlockSpec((None, N_CLASSES, ATT_JB, HEAD_DIM), lambda b, m, n: (n // q_tiles, 0, m, 0))
    vmem = _vmem_limit(
        2 * _nbytes((ATT_TM, D_MODEL), F32), 2 * _nbytes((D_MODEL, QKV_TN), BF16),
        2 * _nbytes((ATT_TM, QKV_TN), F32), _nbytes((ATT_TM, D_MODEL), BF16),
        6 * _nbytes((ATT_TM, HEAD_DIM), F32))
    return pl.pallas_call(
        _qkv_kernel,
        grid=(BATCH, SEQ // ATT_TM, n_out // QKV_TN),
        in_specs=[
            pl.BlockSpec((None, ATT_TM, D_MODEL), lambda b, m, n: (b, m, 0)),
            pl.BlockSpec((1, D_MODEL), lambda b, m, n: (0, 0)),
            pl.BlockSpec(perm.shape, lambda b, m, n: (0, 0)),
            pl.BlockSpec((None, D_MODEL, QKV_TN), lambda b, m, n: (layer, 0, n)),
            tab, tab, tab,
        ],
        out_specs=pl.BlockSpec((None, N_CLASSES, ATT_JB, QKV_TN), lambda b, m, n: (b, 0, m, n)),
        out_shape=jax.ShapeDtypeStruct((BATCH, N_CLASSES, CLASS_LEN, n_out), F32),
        scratch_shapes=[pltpu.VMEM((N_CLASSES, ATT_JB, D_MODEL), BF16)],
        compiler_params=pltpu.CompilerParams(
            dimension_semantics=("parallel", "parallel", "arbitrary"), vmem_limit_bytes=vmem),
        name="attn_qkv",
    )(x, g, perm, w_qkv, *tables)


def _tile_order(kind):
    a = lax.broadcasted_iota(jnp.int32, (SUB_BLOCK, SUB_BLOCK), 0)
    b = lax.broadcasted_iota(jnp.int32, (SUB_BLOCK, SUB_BLOCK), 1)
    if kind == 1:
        f = lambda v: N_CLASSES * (v % SUBLANES) + v // SUBLANES
    else:
        f = lambda v: QCHUNKS * (v % QCHUNK) + v // QCHUNK
    return f(a), f(b)


def _softmax_tiles(tiles):
    ss = [[jnp.where(mk, _dot_nt(q, k), NEG_INF) for k, mk in zip(ks, masks)] for q, ks, _, masks in tiles]
    ms, ps, ls = _softmax_rows(ss)
    accs = [functools.reduce(jnp.add, [jnp.dot(pb, v, preferred_element_type=F32) for pb, v in zip(p, vs)])
            for p, (_, _, vs, _) in zip(ps, tiles)]
    return list(zip(ms, ls, accs))


def _dot_nt(a, b):
    return lax.dot_general(a, b, (((1,), (1,)), ((), ())), preferred_element_type=F32)


def _softmax_rows(ss):
    full = (SUB_BLOCK, SUB_BLOCK)
    ms = [jnp.broadcast_to(functools.reduce(jnp.maximum, s).max(axis=-1, keepdims=True), full) for s in ss]
    ps = [[jnp.exp2(sb - m) for sb in s] for s, m in zip(ss, ms)]
    ls = [jnp.broadcast_to(functools.reduce(jnp.add, p).sum(axis=-1, keepdims=True), full) for p in ps]
    return ms, [[pb.astype(BF16) for pb in p] for p in ps], ls


def _first_pass(groups, cur4, prev4, causal):
    n_blk = CLASS_LEN // QCHUNK
    rows = lambda t, n: t[n * QCHUNK:(n + 1) * QCHUNK]
    regroup = lambda blocks, n: jnp.concatenate([rows(b, n) for b in blocks], axis=0)
    blocks = range(n_blk)
    qt = [[regroup(q_c, n) for n in blocks] for q_c, _, _ in groups]
    kt = [[regroup(k_c, n) for n in blocks] for _, k_c, _ in groups]
    vt = [[regroup(v_c, n) for n in blocks] for _, _, v_c in groups]
    s16 = [[jnp.where(causal, _dot_nt(q, k), NEG_INF) for q, k in zip(q_c, k_c)] for q_c, k_c, _ in groups]
    ss = []
    for g in range(len(groups)):
        for n in blocks:
            s = [jnp.where(cur4, _dot_nt(qt[g][n], kt[g][n]), NEG_INF), regroup(s16[g], n)]
            if n > 0:
                s.append(jnp.where(prev4, _dot_nt(qt[g][n], kt[g][n - 1]), NEG_INF))
            ss.append(s)
    ms, ps, ls = _softmax_rows(ss)
    out = []
    for g, (_, _, v_c) in enumerate(groups):
        tile = lambda n: g * n_blk + n
        pv16 = [jnp.dot(regroup([ps[tile(n)][1] for n in blocks], c), v_c[c], preferred_element_type=F32)
                for c in range(QCHUNKS)]
        accs = []
        for n in blocks:
            acc = jnp.dot(ps[tile(n)][0], vt[g][n], preferred_element_type=F32) + regroup(pv16, n)
            if n > 0:
                acc = acc + jnp.dot(ps[tile(n)][2], vt[g][n - 1], preferred_element_type=F32)
            accs.append(acc)
        out.append([(regroup([ms[tile(n)] for n in blocks], c), regroup([ls[tile(n)] for n in blocks], c),
                     regroup(accs, c)) for c in range(QCHUNKS)])
    return out


def _merge_all(olds, news):
    ms = [jnp.maximum(o[0], n[0]) for o, n in zip(olds, news)]
    a0 = [jnp.exp2(o[0] - m) for o, m in zip(olds, ms)]
    a1 = [jnp.exp2(n[0] - m) for n, m in zip(news, ms)]
    ls = [x * o[1] + y * n[1] for x, y, o, n in zip(a0, a1, olds, news)]
    accs = [x * o[2] + y * n[2] for x, y, o, n in zip(a0, a1, olds, news)]
    return list(zip(ms, ls, accs))


def _attn_kernel(q_ref, k_ref, v_ref, o_ref, acc_ref, m_ref, l_ref):
    ua1, ub1 = _tile_order(1)
    ua4, ub4 = _tile_order(4)
    cur1, cur4, prev4 = ub1 <= ua1, ub4 <= ua4, ub4 >= ua4
    causal = (lax.broadcasted_iota(jnp.int32, (SUB_BLOCK, SUB_BLOCK), 1)
              <= lax.broadcasted_iota(jnp.int32, (SUB_BLOCK, SUB_BLOCK), 0))
    heads = [slice(hh * HEAD_DIM, (hh + 1) * HEAD_DIM) for hh in range(ATT_HEADS)]
    bf = lambda t: t.astype(BF16)

    def first(r4, carry):
        classes = [r4 + QCHUNKS * c for c in range(QCHUNKS)]
        groups = [tuple([bf(ref[r, :, cs]) for r in classes] for ref in (q_ref, k_ref, v_ref)) for cs in heads]
        for hh, per_class in enumerate(_first_pass(groups, cur4, prev4, causal)):
            for r, (m, l, acc) in zip(classes, per_class):
                m_ref[hh, r] = m
                l_ref[hh, r] = l
                acc_ref[hh, r] = acc
        return carry
    lax.fori_loop(0, QCHUNKS, first, 0)

    rows = N_CLASSES * SUBLANES

    def d1(n2, carry):
        start = pl.multiple_of(n2 * 2 * SUBLANES, 2 * SUBLANES)
        prev = pl.multiple_of(jnp.maximum(n2 * 2 - 1, 0) * SUBLANES, SUBLANES)
        prev1 = ub1 >= ua1
        prev1_first = ub1 >= ua1 + jnp.where(n2 > 0, 0, SUB_BLOCK)
        pair = pl.ds(start, 2 * SUBLANES)
        lo, hi = slice(0, SUBLANES), slice(SUBLANES, 2 * SUBLANES)
        flat = lambda t: t.reshape(rows, t.shape[-1])
        tiles, olds = [], []
        for hh, cs in enumerate(heads):
            qb, kb, vb = q_ref[:, pair, cs], k_ref[:, pair, cs], v_ref[:, pair, cs]
            kp, vp = k_ref[:, pl.ds(prev, SUBLANES), cs], v_ref[:, pl.ds(prev, SUBLANES), cs]
            mb, lb, ab = m_ref[hh, :, pair, :], l_ref[hh, :, pair, :], acc_ref[hh, :, pair, :]
            for half, kprev, vprev, pmask in ((lo, kp, vp, prev1_first), (hi, kb[:, lo], vb[:, lo], prev1)):
                tiles.append((bf(flat(qb[:, half])),
                              [bf(flat(kprev)), bf(flat(kb[:, half]))],
                              [bf(flat(vprev)), bf(flat(vb[:, half]))], [pmask, cur1]))
                olds.append((flat(mb[:, half]), flat(lb[:, half]), flat(ab[:, half])))
        outs = [(acc / l).reshape(N_CLASSES, SUBLANES, HEAD_DIM)
                for _, l, acc in _merge_all(olds, _softmax_tiles(tiles))]
        for hh, cs in enumerate(heads):
            o_ref[:, pair, cs] = jnp.concatenate(outs[2 * hh:2 * hh + 2], axis=1).astype(o_ref.dtype)
        return carry
    lax.fori_loop(0, CLASS_LEN // (2 * SUBLANES), d1, 0)


def _attn(qkv):
    blk = (None, N_CLASSES, CLASS_LEN, ATT_HEADS * HEAD_DIM)
    n_hg = N_HEADS // ATT_HEADS
    tile = _nbytes((N_CLASSES, CLASS_LEN, ATT_HEADS * HEAD_DIM), F32)
    state = ATT_HEADS * _nbytes((N_CLASSES, CLASS_LEN, HEAD_DIM), F32)
    vmem = _vmem_limit(6 * tile, tile, 3 * state)
    return pl.pallas_call(
        _attn_kernel,
        grid=(BATCH, n_hg),
        in_specs=[
            pl.BlockSpec(blk, lambda b, g: (b, 0, 0, g)),
            pl.BlockSpec(blk, lambda b, g: (b, 0, 0, n_hg + g)),
            pl.BlockSpec(blk, lambda b, g: (b, 0, 0, 2 * n_hg + g)),
        ],
        out_specs=pl.BlockSpec(blk, lambda b, g: (b, 0, 0, g)),
        out_shape=jax.ShapeDtypeStruct((BATCH, N_CLASSES, CLASS_LEN, D_MODEL), BF16),
        scratch_shapes=[
            pltpu.VMEM((ATT_HEADS, N_CLASSES, CLASS_LEN, HEAD_DIM), F32),
            pltpu.VMEM((ATT_HEADS, N_CLASSES, CLASS_LEN, LANES), F32),
            pltpu.VMEM((ATT_HEADS, N_CLASSES, CLASS_LEN, LANES), F32),
        ],
        compiler_params=pltpu.CompilerParams(
            dimension_semantics=("parallel", "parallel"), vmem_limit_bytes=vmem),
        name="attn_core",
    )(qkv, qkv, qkv)


def _attn_out_kernel(a_ref, p_ref, w_ref, x_ref, o_ref, l_ref):
    @pl.when(pl.program_id(2) == 0)
    def _():
        for g in range(ATT_TM // PERM_GROUP):
            ag = a_ref[:, g * N_CLASSES:(g + 1) * N_CLASSES, :].reshape(PERM_GROUP, D_MODEL)
            l_ref[g * PERM_GROUP:(g + 1) * PERM_GROUP, :] = jnp.dot(
                p_ref[...], ag, preferred_element_type=F32).astype(BF16)

    o_ref[...] = x_ref[...] + jnp.dot(l_ref[...], w_ref[...], preferred_element_type=F32)


def _attn_out(att, w_o, x, perm, layer):
    xblk = pl.BlockSpec((None, ATT_TM, AO_TN), lambda b, m, n: (b, m, n))
    vmem = _vmem_limit(
        3 * _nbytes((ATT_TM, D_MODEL), BF16), 2 * _nbytes((D_MODEL, AO_TN), BF16),
        4 * _nbytes((ATT_TM, AO_TN), F32))
    return pl.pallas_call(
        _attn_out_kernel,
        grid=(BATCH, SEQ // ATT_TM, D_MODEL // AO_TN),
        in_specs=[
            pl.BlockSpec((None, N_CLASSES, ATT_JB, D_MODEL), lambda b, m, n: (b, 0, m, 0)),
            pl.BlockSpec(perm.shape, lambda b, m, n: (0, 0)),
            pl.BlockSpec((None, D_MODEL, AO_TN), lambda b, m, n: (layer, 0, n)),
            xblk,
        ],
        out_specs=xblk,
        out_shape=jax.ShapeDtypeStruct(x.shape, F32),
        scratch_shapes=[pltpu.VMEM((ATT_TM, D_MODEL), BF16)],
        compiler_params=pltpu.CompilerParams(
            dimension_semantics=("parallel", "parallel", "arbitrary"), vmem_limit_bytes=vmem),
        name="attn_out",
    )(att, perm, w_o, x)


def kernel(x, mix_norm, mlp_norm, final_norm, mlp_w1, mlp_w2,
           lru_w_in, lru_conv_w, lru_conv_b, lru_w_a, lru_b_a, lru_w_x, lru_b_x,
           lru_lambda, lru_w_out, attn_w_qkv, attn_w_o):
    assert x.shape == (BATCH, SEQ, D_MODEL) and x.dtype == F32
    row = lambda v: v.reshape(1, -1)
    tables = _rope_tables()
    perm_bt = _swap_perm(BATCH, SUBLANES)
    perm_sq = _swap_perm(N_CLASSES, N_CLASSES)
    gf = row(final_norm)
    (mlp_w1, mlp_w2, lru_w_in, lru_w_a, lru_w_x, lru_w_out, attn_w_qkv, attn_w_o) = (
        w.astype(BF16) for w in (mlp_w1, mlp_w2, lru_w_in, lru_w_a, lru_w_x, lru_w_out, attn_w_qkv, attn_w_o))
    for i in range(DEPTH):
        j = i // N_MIXERS
        g = row(mix_norm[i])
        if i % N_MIXERS == 0:
            xg = _lru_in(x, g, lru_w_in, perm_bt, j)
            x = _lru_tail(xg, lru_conv_w[j], row(lru_conv_b[j]), lru_w_a, row(lru_b_a[j]),
                          lru_w_x, row(lru_b_x[j]), row(lru_lambda[j]), lru_w_out, x, perm_sq, j)
        else:
            qkv = _qkv(x, g, attn_w_qkv, tables, perm_sq, j)
            att = _attn(qkv)
            x = _attn_out(att, attn_w_o, x, perm_sq, j)
        x2d = _mlp(x.reshape(BATCH * SEQ, D_MODEL), row(mlp_norm[i]), mlp_w1, mlp_w2, gf, i == DEPTH - 1, i)
        x = x2d.reshape(BATCH, SEQ, D_MODEL)
    return x
```

```python
import functools
import math

import jax
import jax.numpy as jnp
import numpy as np
from jax import lax
from jax.experimental import pallas as pl
from jax.experimental.pallas import tpu as pltpu

D_MODEL = 2048
BATCH = 16
SEQ = 2048
DEPTH = 4
N_MIXERS = 2
D_RNN = 2560
LRU_BLOCKS = 10
LRU_BLOCK_W = D_RNN // LRU_BLOCKS
CONV_W = 4
LRU_C = 8.0
N_HEADS = 16
HEAD_DIM = D_MODEL // N_HEADS
ROT_DIM = HEAD_DIM // 4
ROPE_THETA = 500000.0
DILATED_PAIRS = ((128, 1), (512, 4), (2048, 16))
SUB_BLOCK = 128
D_FF = 4 * D_MODEL
EPS = 1e-6
NEG_INF = -1e30

V7X_VMEM_BYTES = 64 * 1024 * 1024
V7X_VMEM_RESERVED_BYTES = 4 * 1024 * 1024
MOSAIC_TEMP_BYTES = 10 * 1024 * 1024
SUBLANES = 8
LANES = 128

N_CLASSES = 16
CLASS_LEN = SEQ // N_CLASSES

F32 = jnp.float32
BF16 = jnp.bfloat16
LOG2_E = math.log2(math.e)
GELU_K = math.sqrt(2.0 / math.pi)


def _vmem_limit(*buffer_bytes):
    need = sum(buffer_bytes) + MOSAIC_TEMP_BYTES
    return int(min(need, V7X_VMEM_BYTES - V7X_VMEM_RESERVED_BYTES))


def _nbytes(shape, dtype):
    return int(np.prod(shape)) * jnp.dtype(dtype).itemsize


def _rms_rows(x, g):
    ms = jnp.mean(x * x, axis=-1, keepdims=True)
    return (x * lax.rsqrt(ms + EPS) * g).astype(BF16)


MLP_TM = 1024
MLP_TF = 1024
NORM_CHUNK = 128


def _mlp_kernel(x_ref, g_ref, w1_ref, w2_ref, gf_ref, o_ref, h_ref, *, final_norm):
    f = pl.program_id(1)

    @pl.when(f == 0)
    def _():
        def body(c, carry):
            rows = pl.ds(pl.multiple_of(c * NORM_CHUNK, NORM_CHUNK), NORM_CHUNK)
            x = x_ref[rows, :]
            h_ref[rows, :] = _rms_rows(x, g_ref[...])
            o_ref[rows, :] = x
            return carry
        lax.fori_loop(0, MLP_TM // NORM_CHUNK, body, 0)

    u = jnp.dot(h_ref[...], w1_ref[...], preferred_element_type=F32)
    u = jnp.square(jnp.maximum(u, 0.0)).astype(BF16)
    o_ref[...] += jnp.dot(u, w2_ref[...], preferred_element_type=F32)

    if final_norm:
        @pl.when(f == pl.num_programs(1) - 1)
        def _():
            def body(c, carry):
                rows = pl.ds(pl.multiple_of(c * NORM_CHUNK, NORM_CHUNK), NORM_CHUNK)
                y = o_ref[rows, :]
                ms = jnp.mean(y * y, axis=-1, keepdims=True)
                o_ref[rows, :] = y * lax.rsqrt(ms + EPS) * gf_ref[...]
                return carry
            lax.fori_loop(0, MLP_TM // NORM_CHUNK, body, 0)


def _mlp(x2d, g, w1, w2, gf, final_norm, layer):
    rows = x2d.shape[0]
    vmem = _vmem_limit(
        2 * _nbytes((MLP_TM, D_MODEL), F32), 2 * _nbytes((MLP_TM, D_MODEL), F32),
        _nbytes((MLP_TM, D_MODEL), BF16),
        2 * _nbytes((D_MODEL, MLP_TF), BF16), 2 * _nbytes((MLP_TF, D_MODEL), BF16))
    return pl.pallas_call(
        functools.partial(_mlp_kernel, final_norm=final_norm),
        grid=(rows // MLP_TM, D_FF // MLP_TF),
        in_specs=[
            pl.BlockSpec((MLP_TM, D_MODEL), lambda i, f: (i, 0)),
            pl.BlockSpec((1, D_MODEL), lambda i, f: (0, 0)),
            pl.BlockSpec((None, D_MODEL, MLP_TF), lambda i, f: (layer, 0, f)),
            pl.BlockSpec((None, MLP_TF, D_MODEL), lambda i, f: (layer, f, 0)),
            pl.BlockSpec((1, D_MODEL), lambda i, f: (0, 0)),
        ],
        out_specs=pl.BlockSpec((MLP_TM, D_MODEL), lambda i, f: (i, 0)),
        out_shape=jax.ShapeDtypeStruct(x2d.shape, F32),
        scratch_shapes=[pltpu.VMEM((MLP_TM, D_MODEL), BF16)],
        compiler_params=pltpu.CompilerParams(
            dimension_semantics=("parallel", "arbitrary"), vmem_limit_bytes=vmem),
        name="mlp",
    )(x2d, g, w1, w2, gf)


def _swap_perm(n_a, n_b):
    p = np.zeros((n_a * n_b, n_a * n_b), np.float32)
    a, b = np.meshgrid(np.arange(n_a), np.arange(n_b), indexing="ij")
    p[(b * n_a + a).ravel(), (a * n_b + b).ravel()] = 1.0
    return jnp.asarray(p, dtype=BF16)


LRU_TS = 64
LRU_TN = 1024


def _lru_in_kernel(x_ref, g_ref, p_ref, w_ref, o_ref, h_ref):
    @pl.when(pl.program_id(1) == 0)
    def _():
        grp = BATCH * SUBLANES
        for sg in range(LRU_TS // SUBLANES):
            x = x_ref[:, sg * SUBLANES:(sg + 1) * SUBLANES, :].reshape(grp, D_MODEL)
            hp = jnp.dot(p_ref[...], _rms_rows(x, g_ref[...]), preferred_element_type=F32)
            h_ref[sg * grp:(sg + 1) * grp, :] = hp.astype(BF16)

    res = jnp.dot(h_ref[...], w_ref[...], preferred_element_type=F32)
    o_ref[...] = res.reshape(LRU_TS, BATCH, LRU_TN).astype(o_ref.dtype)


def _lru_in(x, g, w_in, perm, layer):
    n_out = 2 * D_RNN
    vmem = _vmem_limit(
        2 * _nbytes((BATCH, LRU_TS, D_MODEL), F32), 2 * _nbytes((D_MODEL, LRU_TN), BF16),
        2 * _nbytes((LRU_TS, BATCH, LRU_TN), BF16), _nbytes((LRU_TS * BATCH, D_MODEL), BF16))
    return pl.pallas_call(
        _lru_in_kernel,
        grid=(SEQ // LRU_TS, n_out // LRU_TN),
        in_specs=[
            pl.BlockSpec((BATCH, LRU_TS, D_MODEL), lambda i, j: (0, i, 0)),
            pl.BlockSpec((1, D_MODEL), lambda i, j: (0, 0)),
            pl.BlockSpec(perm.shape, lambda i, j: (0, 0)),
            pl.BlockSpec((None, D_MODEL, LRU_TN), lambda i, j: (layer, 0, j)),
        ],
        out_specs=pl.BlockSpec((LRU_TS, BATCH, LRU_TN), lambda i, j: (i, 0, j)),
        out_shape=jax.ShapeDtypeStruct((SEQ, BATCH, n_out), BF16),
        scratch_shapes=[pltpu.VMEM((LRU_TS * BATCH, D_MODEL), BF16)],
        compiler_params=pltpu.CompilerParams(
            dimension_semantics=("parallel", "arbitrary"), vmem_limit_bytes=vmem),
        name="lru_in",
    )(x, g, perm, w_in)


LRU_TAIL_TS = 16
LRU_TAIL_TN = 256


def _lru_tail_kernel(xg_ref, cw_ref, cb_ref, wa_ref, ba_ref, wx_ref, bx_ref, lam_ref, p_ref, w_ref, x_ref,
                     o_ref, tail_ref, h_ref, y_ref, l_ref):
    ts, ct = LRU_TAIL_TS, LRU_BLOCK_W
    rows = ts * BATCH
    pad = CONV_W - 1

    @pl.when(pl.program_id(0) == 0)
    def _():
        tail_ref[...] = jnp.zeros(tail_ref.shape, F32)
        h_ref[...] = jnp.zeros(h_ref.shape, F32)
        y_ref[...] = jnp.zeros(y_ref.shape, BF16)

    yp = jnp.dot(p_ref[...], y_ref[...], preferred_element_type=F32).astype(BF16)
    for b in range(BATCH):
        l_ref[b] = yp[b * ts:(b + 1) * ts]
    lhs = l_ref[...].reshape(rows, D_RNN)

    def project(n):
        cols = slice(n * LRU_TAIL_TN, (n + 1) * LRU_TAIL_TN)
        res = jnp.dot(lhs, w_ref[:, cols], preferred_element_type=F32)
        o_ref[:, :, cols] = x_ref[:, :, cols] + res.reshape(BATCH, ts, LRU_TAIL_TN)

    z = -lam_ref[...]
    decay = (-LRU_C) * (jnp.maximum(z, 0.0) + jnp.log1p(jnp.exp(-jnp.abs(z))))
    sigmoid = lambda t: 1.0 / (1.0 + jnp.exp2(t * (-LOG2_E)))

    def conv_gates(cg):
        cols = slice(cg * ct, (cg + 1) * ct)
        xb = xg_ref[:, :, cols].astype(F32)
        ext = jnp.concatenate([tail_ref[:, :, cols], xb], axis=0)
        tail_ref[:, :, cols] = xb[ts - pad:ts]
        xc = cb_ref[:, cols].reshape(1, 1, ct) + sum(
            cw_ref[k:k + 1, cols].reshape(1, 1, ct) * ext[k:k + ts] for k in range(CONV_W))
        xc2 = xc.reshape(rows, ct)
        xcb = xc2.astype(BF16)
        return (xc2, jnp.dot(xcb, wa_ref[cg], preferred_element_type=F32) + ba_ref[:, cols],
                jnp.dot(xcb, wx_ref[cg], preferred_element_type=F32) + bx_ref[:, cols])

    def recur(cg, xc2, pre_r, pre_i):
        for half in range(ct // LANES):
            lo = half * LANES
            cols = slice(cg * ct + lo, cg * ct + lo + LANES)
            r = sigmoid(pre_r[:, lo:lo + LANES])
            gate = sigmoid(pre_i[:, lo:lo + LANES])
            log_a = r * decay[:, cols]
            a = jnp.exp(log_a).reshape(ts, BATCH, LANES)
            th = jnp.tanh(log_a)
            gap = -2.0 * th / (1.0 - th)
            mult = jnp.where(gap > 0.0, gap * lax.rsqrt(gap), 0.0)
            u = (xc2[:, lo:lo + LANES] * gate * mult).reshape(ts, BATCH, LANES)
            h = h_ref[:, cols]
            hs = []
            for s in range(ts):
                h = a[s] * h + u[s]
                hs.append(h)
            h_ref[:, cols] = h
            gb = xg_ref[:, :, D_RNN + cg * ct + lo:D_RNN + cg * ct + lo + LANES].astype(F32)
            inner = gb * (GELU_K + (0.044715 * GELU_K) * (gb * gb))
            y = (0.5 * jnp.stack(hs)) * (gb * (1.0 + jnp.tanh(inner)))
            y_ref[:, cols] = y.reshape(rows, LANES).astype(BF16)

    n_proj = D_MODEL // LRU_TAIL_TN
    staged = conv_gates(0)
    for k in range(max(LRU_BLOCKS, n_proj)):
        if k < n_proj:
            project(k)
        if k < LRU_BLOCKS:
            ready, staged = staged, (conv_gates(k + 1) if k + 1 < LRU_BLOCKS else None)
            recur(k, *ready)


def _lru_tail(xg, conv_w, conv_b, w_a, b_a, w_x, b_x, lam, w_out, x, perm, layer):
    ts = LRU_TAIL_TS
    n_blk = SEQ // ts
    slab = _nbytes((ts, BATCH, D_RNN), F32)
    vmem = _vmem_limit(
        2 * _nbytes((ts, BATCH, 2 * D_RNN), BF16), 4 * _nbytes((LRU_BLOCKS, LRU_BLOCK_W, LRU_BLOCK_W), BF16),
        2 * _nbytes((D_RNN, D_MODEL), BF16), 4 * _nbytes((BATCH, ts, D_MODEL), F32),
        3 * slab, 3 * _nbytes((ts * BATCH, D_RNN), BF16), 4 * slab)
    vec = pl.BlockSpec((1, D_RNN), lambda i: (0, 0))
    gates = pl.BlockSpec((None, LRU_BLOCKS, LRU_BLOCK_W, LRU_BLOCK_W), lambda i: (layer, 0, 0, 0))
    xblk = pl.BlockSpec((BATCH, ts, D_MODEL), lambda i: (0, jnp.maximum(i - 1, 0), 0))
    return pl.pallas_call(
        _lru_tail_kernel,
        grid=(n_blk + 1,),
        in_specs=[
            pl.BlockSpec((ts, BATCH, 2 * D_RNN), lambda i: (jnp.minimum(i, n_blk - 1), 0, 0)),
            pl.BlockSpec((CONV_W, D_RNN), lambda i: (0, 0)),
            vec, gates, vec, gates, vec, vec,
            pl.BlockSpec(perm.shape, lambda i: (0, 0)),
            pl.BlockSpec((None, D_RNN, D_MODEL), lambda i: (layer, 0, 0)),
            xblk,
        ],
        out_specs=xblk,
        out_shape=jax.ShapeDtypeStruct(x.shape, F32),
        scratch_shapes=[
            pltpu.VMEM((CONV_W - 1, BATCH, D_RNN), F32),
            pltpu.VMEM((BATCH, D_RNN), F32),
            pltpu.VMEM((ts * BATCH, D_RNN), BF16),
            pltpu.VMEM((BATCH, ts, D_RNN), BF16),
        ],
        compiler_params=pltpu.CompilerParams(dimension_semantics=("arbitrary",), vmem_limit_bytes=vmem),
        name="lru_tail",
    )(xg, conv_w, conv_b, w_a, b_a, w_x, b_x, lam, perm, w_out, x)


ATT_TM = 1024
ATT_JB = ATT_TM // N_CLASSES
PERM_GROUP = N_CLASSES * N_CLASSES
QKV_TN = 2048
QKV_CHUNK = 256
ATT_HEADS = 4
AO_TN = 1024
QCHUNKS = 4
QCHUNK = SUB_BLOCK // QCHUNKS
SCORE_SCALE = HEAD_DIM ** -0.5 * math.log2(math.e)


def _rope_tables():
    inv = ROPE_THETA ** (-jnp.arange(0, ROT_DIM, 2, dtype=F32) / ROT_DIM)
    pos = (jnp.arange(CLASS_LEN, dtype=F32)[None, :] * N_CLASSES
           + jnp.arange(N_CLASSES, dtype=F32)[:, None])
    ang = pos[..., None] * inv
    cos, sin = jnp.cos(ang), jnp.sin(ang)
    ones = jnp.ones(pos.shape + (HEAD_DIM - ROT_DIM,), F32)
    zeros_h = jnp.zeros_like(cos)
    zeros_t = jnp.zeros_like(ones)
    c = jnp.concatenate([cos, cos, ones], axis=-1)
    s = jnp.concatenate([-sin, sin, jnp.zeros_like(ones)], axis=-1)
    ident = (jnp.ones_like(c), jnp.zeros_like(c))
    return tuple(jnp.stack([t * SCORE_SCALE, t, e]) for t, e in zip((c, s), ident))


def _qkv_kernel(x_ref, g_ref, p_ref, w_ref, c_ref, s_ref, o_ref, h_ref):
    n = pl.program_id(2)

    @pl.when(n == 0)
    def _():
        for g in range(ATT_TM // PERM_GROUP):
            x = x_ref[g * PERM_GROUP:(g + 1) * PERM_GROUP, :]
            hp = jnp.dot(p_ref[...], _rms_rows(x, g_ref[...]), preferred_element_type=F32).astype(BF16)
            for r in range(N_CLASSES):
                h_ref[r, g * N_CLASSES:(g + 1) * N_CLASSES, :] = hp[r * N_CLASSES:(r + 1) * N_CLASSES]

    hmat = h_ref[...].reshape(ATT_TM, D_MODEL)
    c = c_ref[...].reshape(ATT_TM, HEAD_DIM)
    s = s_ref[...].reshape(ATT_TM, HEAD_DIM)
    first_half = lax.broadcasted_iota(jnp.int32, (ATT_TM, HEAD_DIM), 1) < ROT_DIM // 2
    for ch in range(QKV_TN // QKV_CHUNK):
        res = jnp.dot(hmat, w_ref[:, ch * QKV_CHUNK:(ch + 1) * QKV_CHUNK], preferred_element_type=F32)
        for hh in range(QKV_CHUNK // HEAD_DIM):
            t = res[:, hh * HEAD_DIM:(hh + 1) * HEAD_DIM]
            up = pltpu.roll(t, HEAD_DIM - ROT_DIM // 2, axis=1)
            dn = pltpu.roll(t, ROT_DIM // 2, axis=1)
            col = ch * QKV_CHUNK + hh * HEAD_DIM
            o_ref[:, :, col:col + HEAD_DIM] = (
                t * c + jnp.where(first_half, up, dn) * s).reshape(N_CLASSES, ATT_JB, HEAD_DIM)


def _qkv(x, g, w_qkv, tables, perm, layer):
    n_out = 3 * D_MODEL
    q_tiles = D_MODEL // QKV_TN
    tab = pl.BlockSpec((None, N_CLASSES, ATT_JB, HEAD_DIM), lambda b, m, n: (n // q_tiles, 0, m, 0),
                       pipeline_mode=pl.Buffered(1))
    vmem = _vmem_limit(
        2 * _nbytes((ATT_TM, D_MODEL), F32), 2 * _nbytes((D_MODEL, QKV_TN), BF16),
        2 * _nbytes((ATT_TM, QKV_TN), F32), _nbytes((ATT_TM, D_MODEL), BF16),
        2 * _nbytes((ATT_TM, HEAD_DIM), F32))
    return pl.pallas_call(
        _qkv_kernel,
        grid=(BATCH, SEQ // ATT_TM, n_out // QKV_TN),
        in_specs=[
            pl.BlockSpec((None, ATT_TM, D_MODEL), lambda b, m, n: (b, m, 0)),
            pl.BlockSpec((1, D_MODEL), lambda b, m, n: (0, 0)),
            pl.BlockSpec(perm.shape, lambda b, m, n: (0, 0)),
            pl.BlockSpec((None, D_MODEL, QKV_TN), lambda b, m, n: (layer, 0, n)),
            tab, tab,
        ],
        out_specs=pl.BlockSpec((None, N_CLASSES, ATT_JB, QKV_TN), lambda b, m, n: (b, 0, m, n)),
        out_shape=jax.ShapeDtypeStruct((BATCH, N_CLASSES, CLASS_LEN, n_out), F32),
        scratch_shapes=[pltpu.VMEM((N_CLASSES, ATT_JB, D_MODEL), BF16)],
        compiler_params=pltpu.CompilerParams(
            dimension_semantics=("parallel", "parallel", "arbitrary"), vmem_limit_bytes=vmem),
        name="attn_qkv",
    )(x, g, perm, w_qkv, *tables)


def _tile_order(kind):
    a = lax.broadcasted_iota(jnp.int32, (SUB_BLOCK, SUB_BLOCK), 0)
    b = lax.broadcasted_iota(jnp.int32, (SUB_BLOCK, SUB_BLOCK), 1)
    if kind == 1:
        f = lambda v: N_CLASSES * (v % SUBLANES) + v // SUBLANES
    else:
        f = lambda v: QCHUNKS * (v % QCHUNK) + v // QCHUNK
    return f(a), f(b)


def _softmax_tiles(tiles):
    ss = [[jnp.where(mk, _dot_nt(q, k), NEG_INF) for k, mk in zip(ks, masks)] for q, ks, _, masks in tiles]
    ms, ps, ls = _softmax_rows(ss)
    accs = [functools.reduce(jnp.add, [jnp.dot(pb, v, preferred_element_type=F32) for pb, v in zip(p, vs)])
            for p, (_, _, vs, _) in zip(ps, tiles)]
    return list(zip(ms, ls, accs))


def _dot_nt(a, b):
    return lax.dot_general(a, b, (((1,), (1,)), ((), ())), preferred_element_type=F32)


def _softmax_rows(ss):
    full = (SUB_BLOCK, SUB_BLOCK)
    ms = [jnp.broadcast_to(functools.reduce(jnp.maximum, s).max(axis=-1, keepdims=True), full) for s in ss]
    ps = [[jnp.exp2(sb - m) for sb in s] for s, m in zip(ss, ms)]
    ls = [jnp.broadcast_to(functools.reduce(jnp.add, p).sum(axis=-1, keepdims=True), full) for p in ps]
    return ms, [[pb.astype(BF16) for pb in p] for p in ps], ls


def _first_pass(groups, cur4, prev4, causal):
    n_blk = CLASS_LEN // QCHUNK
    rows = lambda t, n: t[n * QCHUNK:(n + 1) * QCHUNK]
    regroup = lambda blocks, n: jnp.concatenate([rows(b, n) for b in blocks], axis=0)
    blocks = range(n_blk)
    qt = [[regroup(q_c, n) for n in blocks] for q_c, _, _ in groups]
    kt = [[regroup(k_c, n) for n in blocks] for _, k_c, _ in groups]
    vt = [[regroup(v_c, n) for n in blocks] for _, _, v_c in groups]
    s16 = [[jnp.where(causal, _dot_nt(q, k), NEG_INF) for q, k in zip(q_c, k_c)] for q_c, k_c, _ in groups]
    ss = []
    for g in range(len(groups)):
        for n in blocks:
            s = [jnp.where(cur4, _dot_nt(qt[g][n], kt[g][n]), NEG_INF), regroup(s16[g], n)]
            if n > 0:
                s.append(jnp.where(prev4, _dot_nt(qt[g][n], kt[g][n - 1]), NEG_INF))
            ss.append(s)
    ms, ps, ls = _softmax_rows(ss)
    out = []
    for g, (_, _, v_c) in enumerate(groups):
        tile = lambda n: g * n_blk + n
        pv16 = [jnp.dot(regroup([ps[tile(n)][1] for n in blocks], c), v_c[c], preferred_element_type=F32)
                for c in range(QCHUNKS)]
        accs = []
        for n in blocks:
            acc = jnp.dot(ps[tile(n)][0], vt[g][n], preferred_element_type=F32) + regroup(pv16, n)
            if n > 0:
                acc = acc + jnp.dot(ps[tile(n)][2], vt[g][n - 1], preferred_element_type=F32)
            accs.append(acc)
        out.append([(regroup([ms[tile(n)] for n in blocks], c), regroup([ls[tile(n)] for n in blocks], c),
                     regroup(accs, c)) for c in range(QCHUNKS)])
    return out


def _merge_all(olds, news):
    ms = [jnp.maximum(o[0], n[0]) for o, n in zip(olds, news)]
    a0 = [jnp.exp2(o[0] - m) for o, m in zip(olds, ms)]
    a1 = [jnp.exp2(n[0] - m) for n, m in zip(news, ms)]
    ls = [x * o[1] + y * n[1] for x, y, o, n in zip(a0, a1, olds, news)]
    accs = [x * o[2] + y * n[2] for x, y, o, n in zip(a0, a1, olds, news)]
    return list(zip(ms, ls, accs))


def _attn_kernel(q_ref, k_ref, v_ref, o_ref, acc_ref, m_ref, l_ref):
    ua1, ub1 = _tile_order(1)
    ua4, ub4 = _tile_order(4)
    cur1, cur4, prev4 = ub1 <= ua1, ub4 <= ua4, ub4 >= ua4
    causal = (lax.broadcasted_iota(jnp.int32, (SUB_BLOCK, SUB_BLOCK), 1)
              <= lax.broadcasted_iota(jnp.int32, (SUB_BLOCK, SUB_BLOCK), 0))
    heads = [slice(hh * HEAD_DIM, (hh + 1) * HEAD_DIM) for hh in range(ATT_HEADS)]
    bf = lambda t: t.astype(BF16)

    def first(r4, carry):
        classes = [r4 + QCHUNKS * c for c in range(QCHUNKS)]
        groups = [tuple([bf(ref[r, :, cs]) for r in classes] for ref in (q_ref, k_ref, v_ref)) for cs in heads]
        for hh, per_class in enumerate(_first_pass(groups, cur4, prev4, causal)):
            for r, (m, l, acc) in zip(classes, per_class):
                m_ref[hh, r] = m
                l_ref[hh, r] = l
                acc_ref[hh, r] = acc
        return carry
    lax.fori_loop(0, QCHUNKS, first, 0)

    rows = N_CLASSES * SUBLANES

    def d1(n2, carry):
        start = pl.multiple_of(n2 * 2 * SUBLANES, 2 * SUBLANES)
        prev = pl.multiple_of(jnp.maximum(n2 * 2 - 1, 0) * SUBLANES, SUBLANES)
        prev1 = ub1 >= ua1
        prev1_first = ub1 >= ua1 + jnp.where(n2 > 0, 0, SUB_BLOCK)
        pair = pl.ds(start, 2 * SUBLANES)
        lo, hi = slice(0, SUBLANES), slice(SUBLANES, 2 * SUBLANES)
        flat = lambda t: t.reshape(rows, t.shape[-1])
        tiles, olds = [], []
        for hh, cs in enumerate(heads):
            qb, kb, vb = q_ref[:, pair, cs], k_ref[:, pair, cs], v_ref[:, pair, cs]
            kp, vp = k_ref[:, pl.ds(prev, SUBLANES), cs], v_ref[:, pl.ds(prev, SUBLANES), cs]
            mb, lb, ab = m_ref[hh, :, pair, :], l_ref[hh, :, pair, :], acc_ref[hh, :, pair, :]
            for half, kprev, vprev, pmask in ((lo, kp, vp, prev1_first), (hi, kb[:, lo], vb[:, lo], prev1)):
                tiles.append((bf(flat(qb[:, half])),
                              [bf(flat(kprev)), bf(flat(kb[:, half]))],
                              [bf(flat(vprev)), bf(flat(vb[:, half]))], [pmask, cur1]))
                olds.append((flat(mb[:, half]), flat(lb[:, half]), flat(ab[:, half])))
        outs = [(acc / l).reshape(N_CLASSES, SUBLANES, HEAD_DIM)
                for _, l, acc in _merge_all(olds, _softmax_tiles(tiles))]
        for hh, cs in enumerate(heads):
            o_ref[:, pair, cs] = jnp.concatenate(outs[2 * hh:2 * hh + 2], axis=1).astype(o_ref.dtype)
        return carry
    lax.fori_loop(0, CLASS_LEN // (2 * SUBLANES), d1, 0)


def _attn(qkv):
    blk = (None, N_CLASSES, CLASS_LEN, ATT_HEADS * HEAD_DIM)
    n_hg = N_HEADS // ATT_HEADS
    tile = _nbytes((N_CLASSES, CLASS_LEN, ATT_HEADS * HEAD_DIM), F32)
    state = ATT_HEADS * _nbytes((N_CLASSES, CLASS_LEN, HEAD_DIM), F32)
    vmem = _vmem_limit(6 * tile, tile, 3 * state)
    return pl.pallas_call(
        _attn_kernel,
        grid=(BATCH, n_hg),
        in_specs=[
            pl.BlockSpec(blk, lambda b, g: (b, 0, 0, g)),
            pl.BlockSpec(blk, lambda b, g: (b, 0, 0, n_hg + g)),
            pl.BlockSpec(blk, lambda b, g: (b, 0, 0, 2 * n_hg + g)),
        ],
        out_specs=pl.BlockSpec(blk, lambda b, g: (b, 0, 0, g)),
        out_shape=jax.ShapeDtypeStruct((BATCH, N_CLASSES, CLASS_LEN, D_MODEL), BF16),
        scratch_shapes=[
            pltpu.VMEM((ATT_HEADS, N_CLASSES, CLASS_LEN, HEAD_DIM), F32),
            pltpu.VMEM((ATT_HEADS, N_CLASSES, CLASS_LEN, LANES), F32),
            pltpu.VMEM((ATT_HEADS, N_CLASSES, CLASS_LEN, LANES), F32),
        ],
        compiler_params=pltpu.CompilerParams(
            dimension_semantics=("parallel", "parallel"), vmem_limit_bytes=vmem),
        name="attn_core",
    )(qkv, qkv, qkv)


def _attn_out_kernel(a_ref, p_ref, w_ref, x_ref, o_ref, l_ref):
    @pl.when(pl.program_id(2) == 0)
    def _():
        for g in range(ATT_TM // PERM_GROUP):
            ag = a_ref[:, g * N_CLASSES:(g + 1) * N_CLASSES, :].reshape(PERM_GROUP, D_MODEL)
            l_ref[g * PERM_GROUP:(g + 1) * PERM_GROUP, :] = jnp.dot(
                p_ref[...], ag, preferred_element_type=F32).astype(BF16)

    o_ref[...] = x_ref[...] + jnp.dot(l_ref[...], w_ref[...], preferred_element_type=F32)


def _attn_out(att, w_o, x, perm, layer):
    xblk = pl.BlockSpec((None, ATT_TM, AO_TN), lambda b, m, n: (b, m, n))
    vmem = _vmem_limit(
        3 * _nbytes((ATT_TM, D_MODEL), BF16), 2 * _nbytes((D_MODEL, AO_TN), BF16),
        4 * _nbytes((ATT_TM, AO_TN), F32))
    return pl.pallas_call(
        _attn_out_kernel,
        grid=(BATCH, SEQ // ATT_TM, D_MODEL // AO_TN),
        in_specs=[
            pl.BlockSpec((None, N_CLASSES, ATT_JB, D_MODEL), lambda b, m, n: (b, 0, m, 0)),
            pl.BlockSpec(perm.shape, lambda b, m, n: (0, 0)),
            pl.BlockSpec((None, D_MODEL, AO_TN), lambda b, m, n: (layer, 0, n)),
            xblk,
        ],
        out_specs=xblk,
        out_shape=jax.ShapeDtypeStruct(x.shape, F32),
        scratch_shapes=[pltpu.VMEM((ATT_TM, D_MODEL), BF16)],
        compiler_params=pltpu.CompilerParams(
            dimension_semantics=("parallel", "parallel", "arbitrary"), vmem_limit_bytes=vmem),
        name="attn_out",
    )(att, perm, w_o, x)


def kernel(x, mix_norm, mlp_norm, final_norm, mlp_w1, mlp_w2,
           lru_w_in, lru_conv_w, lru_conv_b, lru_w_a, lru_b_a, lru_w_x, lru_b_x,
           lru_lambda, lru_w_out, attn_w_qkv, attn_w_o):
    assert x.shape == (BATCH, SEQ, D_MODEL) and x.dtype == F32
    row = lambda v: v.reshape(1, -1)
    tables = _rope_tables()
    perm_bt = _swap_perm(BATCH, SUBLANES)
    perm_sq = _swap_perm(N_CLASSES, N_CLASSES)
    gf = row(final_norm)
    (mlp_w1, mlp_w2, lru_w_in, lru_w_a, lru_w_x, lru_w_out, attn_w_qkv, attn_w_o) = (
        w.astype(BF16) for w in (mlp_w1, mlp_w2, lru_w_in, lru_w_a, lru_w_x, lru_w_out, attn_w_qkv, attn_w_o))
    for i in range(DEPTH):
        j = i // N_MIXERS
        g = row(mix_norm[i])
        if i % N_MIXERS == 0:
            xg = _lru_in(x, g, lru_w_in, perm_bt, j)
            x = _lru_tail(xg, lru_conv_w[j], row(lru_conv_b[j]), lru_w_a, row(lru_b_a[j]),
                          lru_w_x, row(lru_b_x[j]), row(lru_lambda[j]), lru_w_out, x, perm_sq, j)
        else:
            qkv = _qkv(x, g, attn_w_qkv, tables, perm_sq, j)
            att = _attn(qkv)
            x = _attn_out(att, attn_w_o, x, perm_sq, j)
        x2d = _mlp(x.reshape(BATCH * SEQ, D_MODEL), row(mlp_norm[i]), mlp_w1, mlp_w2, gf, i == DEPTH - 1, i)
        x = x2d.reshape(BATCH, SEQ, D_MODEL)
    return x
```

```python
import functools
import math

import jax
import jax.numpy as jnp
import numpy as np
from jax import lax
from jax.experimental import pallas as pl
from jax.experimental.pallas import tpu as pltpu

D_MODEL = 2048
BATCH = 16
SEQ = 2048
DEPTH = 4
N_MIXERS = 2
D_RNN = 2560
LRU_BLOCKS = 10
LRU_BLOCK_W = D_RNN // LRU_BLOCKS
CONV_W = 4
LRU_C = 8.0
N_HEADS = 16
HEAD_DIM = D_MODEL // N_HEADS
ROT_DIM = HEAD_DIM // 4
ROPE_THETA = 500000.0
DILATED_PAIRS = ((128, 1), (512, 4), (2048, 16))
SUB_BLOCK = 128
D_FF = 4 * D_MODEL
EPS = 1e-6
NEG_INF = -1e30

V7X_VMEM_BYTES = 64 * 1024 * 1024
V7X_VMEM_RESERVED_BYTES = 4 * 1024 * 1024
MOSAIC_TEMP_BYTES = 10 * 1024 * 1024
SUBLANES = 8
LANES = 128

N_CLASSES = 16
CLASS_LEN = SEQ // N_CLASSES

F32 = jnp.float32
BF16 = jnp.bfloat16
LOG2_E = math.log2(math.e)
GELU_K = math.sqrt(2.0 / math.pi)


def _vmem_limit(*buffer_bytes):
    need = sum(buffer_bytes) + MOSAIC_TEMP_BYTES
    return int(min(need, V7X_VMEM_BYTES - V7X_VMEM_RESERVED_BYTES))


def _nbytes(shape, dtype):
    return int(np.prod(shape)) * jnp.dtype(dtype).itemsize


def _rms_rows(x, g):
    ms = jnp.mean(x * x, axis=-1, keepdims=True)
    return (x * lax.rsqrt(ms + EPS) * g).astype(BF16)


MLP_TM = 1024
MLP_TF = 1024
NORM_CHUNK = 128


def _mlp_kernel(x_ref, g_ref, w1_ref, w2_ref, gf_ref, o_ref, h_ref, *, final_norm):
    f = pl.program_id(1)

    @pl.when(f == 0)
    def _():
        def body(c, carry):
            rows = pl.ds(pl.multiple_of(c * NORM_CHUNK, NORM_CHUNK), NORM_CHUNK)
            x = x_ref[rows, :]
            h_ref[rows, :] = _rms_rows(x, g_ref[...])
            o_ref[rows, :] = x
            return carry
        lax.fori_loop(0, MLP_TM // NORM_CHUNK, body, 0)

    u = jnp.dot(h_ref[...], w1_ref[...], preferred_element_type=F32)
    u = jnp.square(jnp.maximum(u, 0.0)).astype(BF16)
    o_ref[...] += jnp.dot(u, w2_ref[...], preferred_element_type=F32)

    if final_norm:
        @pl.when(f == pl.num_programs(1) - 1)
        def _():
            def body(c, carry):
                rows = pl.ds(pl.multiple_of(c * NORM_CHUNK, NORM_CHUNK), NORM_CHUNK)
                y = o_ref[rows, :]
                ms = jnp.mean(y * y, axis=-1, keepdims=True)
                o_ref[rows, :] = y * lax.rsqrt(ms + EPS) * gf_ref[...]
                return carry
            lax.fori_loop(0, MLP_TM // NORM_CHUNK, body, 0)


def _mlp(x2d, g, w1, w2, gf, final_norm, layer):
    rows = x2d.shape[0]
    vmem = _vmem_limit(
        2 * _nbytes((MLP_TM, D_MODEL), F32), 2 * _nbytes((MLP_TM, D_MODEL), F32),
        _nbytes((MLP_TM, D_MODEL), BF16),
        2 * _nbytes((D_MODEL, MLP_TF), BF16), 2 * _nbytes((MLP_TF, D_MODEL), BF16))
    return pl.pallas_call(
        functools.partial(_mlp_kernel, final_norm=final_norm),
        grid=(rows // MLP_TM, D_FF // MLP_TF),
        in_specs=[
            pl.BlockSpec((MLP_TM, D_MODEL), lambda i, f: (i, 0)),
            pl.BlockSpec((1, D_MODEL), lambda i, f: (0, 0)),
            pl.BlockSpec((None, D_MODEL, MLP_TF), lambda i, f: (layer, 0, f)),
            pl.BlockSpec((None, MLP_TF, D_MODEL), lambda i, f: (layer, f, 0)),
            pl.BlockSpec((1, D_MODEL), lambda i, f: (0, 0)),
        ],
        out_specs=pl.BlockSpec((MLP_TM, D_MODEL), lambda i, f: (i, 0)),
        out_shape=jax.ShapeDtypeStruct(x2d.shape, F32),
        scratch_shapes=[pltpu.VMEM((MLP_TM, D_MODEL), BF16)],
        compiler_params=pltpu.CompilerParams(
            dimension_semantics=("parallel", "arbitrary"), vmem_limit_bytes=vmem),
        name="mlp",
    )(x2d, g, w1, w2, gf)


def _swap_perm(n_a, n_b):
    p = np.zeros((n_a * n_b, n_a * n_b), np.float32)
    a, b = np.meshgrid(np.arange(n_a), np.arange(n_b), indexing="ij")
    p[(b * n_a + a).ravel(), (a * n_b + b).ravel()] = 1.0
    return jnp.asarray(p, dtype=BF16)


LRU_TS = 32
LRU_TN = 512


def _lru_in_kernel(x_ref, g_ref, p_ref, w_ref, o_ref, h_ref):
    grp = BATCH * SUBLANES
    for sg in range(LRU_TS // SUBLANES):
        x = x_ref[:, sg * SUBLANES:(sg + 1) * SUBLANES, :].reshape(grp, D_MODEL)
        hp = jnp.dot(p_ref[...], _rms_rows(x, g_ref[...]), preferred_element_type=F32)
        h_ref[sg * grp:(sg + 1) * grp, :] = hp.astype(BF16)
    h = h_ref[...]
    for ch in range(2 * D_RNN // LRU_TN):
        cols = slice(ch * LRU_TN, (ch + 1) * LRU_TN)
        res = jnp.dot(h, w_ref[:, cols], preferred_element_type=F32)
        o_ref[:, :, cols] = res.reshape(LRU_TS, BATCH, LRU_TN).astype(o_ref.dtype)


def _lru_in(x, g, w_in, perm, layer):
    n_out = 2 * D_RNN
    vmem = _vmem_limit(
        2 * _nbytes((BATCH, LRU_TS, D_MODEL), F32), _nbytes((D_MODEL, n_out), BF16),
        2 * _nbytes((LRU_TS, BATCH, n_out), BF16), _nbytes((LRU_TS * BATCH, D_MODEL), BF16))
    return pl.pallas_call(
        _lru_in_kernel,
        grid=(SEQ // LRU_TS,),
        in_specs=[
            pl.BlockSpec((BATCH, LRU_TS, D_MODEL), lambda i: (0, i, 0)),
            pl.BlockSpec((1, D_MODEL), lambda i: (0, 0)),
            pl.BlockSpec(perm.shape, lambda i: (0, 0)),
            pl.BlockSpec((None, D_MODEL, n_out), lambda i: (layer, 0, 0), pipeline_mode=pl.Buffered(1)),
        ],
        out_specs=pl.BlockSpec((LRU_TS, BATCH, n_out), lambda i: (i, 0, 0)),
        out_shape=jax.ShapeDtypeStruct((SEQ, BATCH, n_out), BF16),
        scratch_shapes=[pltpu.VMEM((LRU_TS * BATCH, D_MODEL), BF16)],
        compiler_params=pltpu.CompilerParams(
            dimension_semantics=("parallel",), vmem_limit_bytes=vmem),
        name="lru_in",
    )(x, g, perm, w_in)


LRU_TAIL_TS = 16
LRU_TAIL_TN = 256


def _lru_tail_kernel(xg_ref, cw_ref, cb_ref, wa_ref, ba_ref, wx_ref, bx_ref, lam_ref, p_ref, w_ref, x_ref,
                     o_ref, tail_ref, h_ref, y_ref, l_ref):
    ts, ct = LRU_TAIL_TS, LRU_BLOCK_W
    rows = ts * BATCH
    pad = CONV_W - 1

    @pl.when(pl.program_id(0) == 0)
    def _():
        tail_ref[...] = jnp.zeros(tail_ref.shape, F32)
        h_ref[...] = jnp.zeros(h_ref.shape, F32)
        y_ref[...] = jnp.zeros(y_ref.shape, BF16)

    yp = jnp.dot(p_ref[...], y_ref[...], preferred_element_type=F32).astype(BF16)
    for b in range(BATCH):
        l_ref[b] = yp[b * ts:(b + 1) * ts]
    lhs = l_ref[...].reshape(rows, D_RNN)

    def project(n):
        cols = slice(n * LRU_TAIL_TN, (n + 1) * LRU_TAIL_TN)
        res = jnp.dot(lhs, w_ref[:, cols], preferred_element_type=F32)
        o_ref[:, :, cols] = x_ref[:, :, cols] + res.reshape(BATCH, ts, LRU_TAIL_TN)

    z = -lam_ref[...]
    decay = (-LRU_C) * (jnp.maximum(z, 0.0) + jnp.log1p(jnp.exp(-jnp.abs(z))))
    sigmoid = lambda t: 1.0 / (1.0 + jnp.exp2(t * (-LOG2_E)))

    def conv_gates(cg):
        cols = slice(cg * ct, (cg + 1) * ct)
        xb = xg_ref[:, :, cols].astype(F32)
        ext = jnp.concatenate([tail_ref[:, :, cols], xb], axis=0)
        tail_ref[:, :, cols] = xb[ts - pad:ts]
        xc = cb_ref[:, cols].reshape(1, 1, ct) + sum(
            cw_ref[k:k + 1, cols].reshape(1, 1, ct) * ext[k:k + ts] for k in range(CONV_W))
        xc2 = xc.reshape(rows, ct)
        xcb = xc2.astype(BF16)
        return (xc2, jnp.dot(xcb, wa_ref[cg], preferred_element_type=F32) + ba_ref[:, cols],
                jnp.dot(xcb, wx_ref[cg], preferred_element_type=F32) + bx_ref[:, cols])

    def recur(cg, xc2, pre_r, pre_i):
        for half in range(ct // LANES):
            lo = half * LANES
            cols = slice(cg * ct + lo, cg * ct + lo + LANES)
            r = sigmoid(pre_r[:, lo:lo + LANES])
            gate = sigmoid(pre_i[:, lo:lo + LANES])
            log_a = r * decay[:, cols]
            a = jnp.exp(log_a).reshape(ts, BATCH, LANES)
            th = jnp.tanh(log_a)
            gap = -2.0 * th / (1.0 - th)
            mult = jnp.where(gap > 0.0, gap * lax.rsqrt(gap), 0.0)
            u = (xc2[:, lo:lo + LANES] * gate * mult).reshape(ts, BATCH, LANES)
            h = h_ref[:, cols]
            hs = []
            for s in range(ts):
                h = a[s] * h + u[s]
                hs.append(h)
            h_ref[:, cols] = h
            gb = xg_ref[:, :, D_RNN + cg * ct + lo:D_RNN + cg * ct + lo + LANES].astype(F32)
            inner = gb * (GELU_K + (0.044715 * GELU_K) * (gb * gb))
            y = (0.5 * jnp.stack(hs)) * (gb * (1.0 + jnp.tanh(inner)))
            y_ref[:, cols] = y.reshape(rows, LANES).astype(BF16)

    n_proj = D_MODEL // LRU_TAIL_TN
    staged = conv_gates(0)
    for k in range(max(LRU_BLOCKS, n_proj)):
        if k < n_proj:
            project(k)
        if k < LRU_BLOCKS:
            ready, staged = staged, (conv_gates(k + 1) if k + 1 < LRU_BLOCKS else None)
            recur(k, *ready)


def _lru_tail(xg, conv_w, conv_b, w_a, b_a, w_x, b_x, lam, w_out, x, perm, layer):
    ts = LRU_TAIL_TS
    n_blk = SEQ // ts
    slab = _nbytes((ts, BATCH, D_RNN), F32)
    vmem = _vmem_limit(
        2 * _nbytes((ts, BATCH, 2 * D_RNN), BF16), 4 * _nbytes((LRU_BLOCKS, LRU_BLOCK_W, LRU_BLOCK_W), BF16),
        2 * _nbytes((D_RNN, D_MODEL), BF16), 4 * _nbytes((BATCH, ts, D_MODEL), F32),
        3 * slab, 3 * _nbytes((ts * BATCH, D_RNN), BF16), 4 * slab)
    vec = pl.BlockSpec((1, D_RNN), lambda i: (0, 0))
    gates = pl.BlockSpec((None, LRU_BLOCKS, LRU_BLOCK_W, LRU_BLOCK_W), lambda i: (layer, 0, 0, 0))
    xblk = pl.BlockSpec((BATCH, ts, D_MODEL), lambda i: (0, jnp.maximum(i - 1, 0), 0))
    return pl.pallas_call(
        _lru_tail_kernel,
        grid=(n_blk + 1,),
        in_specs=[
            pl.BlockSpec((ts, BATCH, 2 * D_RNN), lambda i: (jnp.minimum(i, n_blk - 1), 0, 0)),
            pl.BlockSpec((CONV_W, D_RNN), lambda i: (0, 0)),
            vec, gates, vec, gates, vec, vec,
            pl.BlockSpec(perm.shape, lambda i: (0, 0)),
            pl.BlockSpec((None, D_RNN, D_MODEL), lambda i: (layer, 0, 0)),
            xblk,
        ],
        out_specs=xblk,
        out_shape=jax.ShapeDtypeStruct(x.shape, F32),
        scratch_shapes=[
            pltpu.VMEM((CONV_W - 1, BATCH, D_RNN), F32),
            pltpu.VMEM((BATCH, D_RNN), F32),
            pltpu.VMEM((ts * BATCH, D_RNN), BF16),
            pltpu.VMEM((BATCH, ts, D_RNN), BF16),
        ],
        compiler_params=pltpu.CompilerParams(dimension_semantics=("arbitrary",), vmem_limit_bytes=vmem),
        name="lru_tail",
    )(xg, conv_w, conv_b, w_a, b_a, w_x, b_x, lam, perm, w_out, x)


ATT_TM = 1024
ATT_JB = ATT_TM // N_CLASSES
PERM_GROUP = N_CLASSES * N_CLASSES
QKV_TN = 1024
QKV_CHUNK = 256
ATT_HEADS = 4
AO_TN = 1024
QCHUNKS = 4
QCHUNK = SUB_BLOCK // QCHUNKS
SCORE_SCALE = HEAD_DIM ** -0.5 * math.log2(math.e)


def _rope_tables():
    inv = ROPE_THETA ** (-jnp.arange(0, ROT_DIM, 2, dtype=F32) / ROT_DIM)
    pos = (jnp.arange(CLASS_LEN, dtype=F32)[None, :] * N_CLASSES
           + jnp.arange(N_CLASSES, dtype=F32)[:, None])
    ang = pos[..., None] * inv
    cos, sin = jnp.cos(ang), jnp.sin(ang)
    ones = jnp.ones(pos.shape + (HEAD_DIM - ROT_DIM,), F32)
    zeros_h = jnp.zeros_like(cos)
    zeros_t = jnp.zeros_like(ones)
    c = jnp.concatenate([cos, cos, ones], axis=-1)
    s_up = jnp.concatenate([-sin, zeros_h, zeros_t], axis=-1)
    s_dn = jnp.concatenate([zeros_h, sin, zeros_t], axis=-1)
    ident = (jnp.ones_like(c), jnp.zeros_like(c), jnp.zeros_like(c))
    return tuple(jnp.stack([t * SCORE_SCALE, t, e]) for t, e in zip((c, s_up, s_dn), ident))


def _qkv_kernel(x_ref, g_ref, p_ref, w_ref, c_ref, su_ref, sd_ref, o_ref, h_ref):
    n = pl.program_id(2)

    @pl.when(n == 0)
    def _():
        for g in range(ATT_TM // PERM_GROUP):
            x = x_ref[g * PERM_GROUP:(g + 1) * PERM_GROUP, :]
            hp = jnp.dot(p_ref[...], _rms_rows(x, g_ref[...]), preferred_element_type=F32).astype(BF16)
            for r in range(N_CLASSES):
                h_ref[r, g * N_CLASSES:(g + 1) * N_CLASSES, :] = hp[r * N_CLASSES:(r + 1) * N_CLASSES]

    hmat = h_ref[...].reshape(ATT_TM, D_MODEL)
    c = c_ref[...].reshape(ATT_TM, HEAD_DIM)
    su = su_ref[...].reshape(ATT_TM, HEAD_DIM)
    sd = sd_ref[...].reshape(ATT_TM, HEAD_DIM)
    for ch in range(QKV_TN // QKV_CHUNK):
        res = jnp.dot(hmat, w_ref[:, ch * QKV_CHUNK:(ch + 1) * QKV_CHUNK], preferred_element_type=F32)
        for hh in range(QKV_CHUNK // HEAD_DIM):
            t = res[:, hh * HEAD_DIM:(hh + 1) * HEAD_DIM]
            up = pltpu.roll(t, HEAD_DIM - ROT_DIM // 2, axis=1)
            dn = pltpu.roll(t, ROT_DIM // 2, axis=1)
            col = ch * QKV_CHUNK + hh * HEAD_DIM
            o_ref[:, :, col:col + HEAD_DIM] = (t * c + up * su + dn * sd).reshape(N_CLASSES, ATT_JB, HEAD_DIM)


def _qkv(x, g, w_qkv, tables, perm, layer):
    n_out = 3 * D_MODEL
    q_tiles = D_MODEL // QKV_TN
    tab = pl.BlockSpec((None, N_CLASSES, ATT_JB, HEAD_DIM), lambda b, m, n: (n // q_tiles, 0, m, 0))
    vmem = _vmem_limit(
        2 * _nbytes((ATT_TM, D_MODEL), F32), 2 * _nbytes((D_MODEL, QKV_TN), BF16),
        2 * _nbytes((ATT_TM, QKV_TN), F32), _nbytes((ATT_TM, D_MODEL), BF16),
        6 * _nbytes((ATT_TM, HEAD_DIM), F32))
    return pl.pallas_call(
        _qkv_kernel,
        grid=(BATCH, SEQ // ATT_TM, n_out // QKV_TN),
        in_specs=[
            pl.BlockSpec((None, ATT_TM, D_MODEL), lambda b, m, n: (b, m, 0)),
            pl.BlockSpec((1, D_MODEL), lambda b, m, n: (0, 0)),
            pl.BlockSpec(perm.shape, lambda b, m, n: (0, 0)),
            pl.BlockSpec((None, D_MODEL, QKV_TN), lambda b, m, n: (layer, 0, n)),
            tab, tab, tab,
        ],
        out_specs=pl.BlockSpec((None, N_CLASSES, ATT_JB, QKV_TN), lambda b, m, n: (b, 0, m, n)),
        out_shape=jax.ShapeDtypeStruct((BATCH, N_CLASSES, CLASS_LEN, n_out), F32),
        scratch_shapes=[pltpu.VMEM((N_CLASSES, ATT_JB, D_MODEL), BF16)],
        compiler_params=pltpu.CompilerParams(
            dimension_semantics=("parallel", "parallel", "arbitrary"), vmem_limit_bytes=vmem),
        name="attn_qkv",
    )(x, g, perm, w_qkv, *tables)


def _tile_order(kind):
    a = lax.broadcasted_iota(jnp.int32, (SUB_BLOCK, SUB_BLOCK), 0)
    b = lax.broadcasted_iota(jnp.int32, (SUB_BLOCK, SUB_BLOCK), 1)
    if kind == 1:
        f = lambda v: N_CLASSES * (v % SUBLANES) + v // SUBLANES
    else:
        f = lambda v: QCHUNKS * (v % QCHUNK) + v // QCHUNK
    return f(a), f(b)


def _softmax_tiles(tiles):
    ss = [[jnp.where(mk, _dot_nt(q, k), NEG_INF) for k, mk in zip(ks, masks)] for q, ks, _, masks in tiles]
    ms, ps, ls = _softmax_rows(ss)
    accs = [functools.reduce(jnp.add, [jnp.dot(pb, v, preferred_element_type=F32) for pb, v in zip(p, vs)])
            for p, (_, _, vs, _) in zip(ps, tiles)]
    return list(zip(ms, ls, accs))


def _dot_nt(a, b):
    return lax.dot_general(a, b, (((1,), (1,)), ((), ())), preferred_element_type=F32)


def _softmax_rows(ss):
    full = (SUB_BLOCK, SUB_BLOCK)
    ms = [jnp.broadcast_to(functools.reduce(jnp.maximum, s).max(axis=-1, keepdims=True), full) for s in ss]
    ps = [[jnp.exp2(sb - m) for sb in s] for s, m in zip(ss, ms)]
    ls = [jnp.broadcast_to(functools.reduce(jnp.add, p).sum(axis=-1, keepdims=True), full) for p in ps]
    return ms, [[pb.astype(BF16) for pb in p] for p in ps], ls


def _first_pass(groups, cur4, prev4, causal):
    n_blk = CLASS_LEN // QCHUNK
    rows = lambda t, n: t[n * QCHUNK:(n + 1) * QCHUNK]
    regroup = lambda blocks, n: jnp.concatenate([rows(b, n) for b in blocks], axis=0)
    blocks = range(n_blk)
    qt = [[regroup(q_c, n) for n in blocks] for q_c, _, _ in groups]
    kt = [[regroup(k_c, n) for n in blocks] for _, k_c, _ in groups]
    vt = [[regroup(v_c, n) for n in blocks] for _, _, v_c in groups]
    s16 = [[jnp.where(causal, _dot_nt(q, k), NEG_INF) for q, k in zip(q_c, k_c)] for q_c, k_c, _ in groups]
    ss = []
    for g in range(len(groups)):
        for n in blocks:
            s = [jnp.where(cur4, _dot_nt(qt[g][n], kt[g][n]), NEG_INF), regroup(s16[g], n)]
            if n > 0:
                s.append(jnp.where(prev4, _dot_nt(qt[g][n], kt[g][n - 1]), NEG_INF))
            ss.append(s)
    ms, ps, ls = _softmax_rows(ss)
    out = []
    for g, (_, _, v_c) in enumerate(groups):
        tile = lambda n: g * n_blk + n
        pv16 = [jnp.dot(regroup([ps[tile(n)][1] for n in blocks], c), v_c[c], preferred_element_type=F32)
                for c in range(QCHUNKS)]
        accs = []
        for n in blocks:
            acc = jnp.dot(ps[tile(n)][0], vt[g][n], preferred_element_type=F32) + regroup(pv16, n)
            if n > 0:
                acc = acc + jnp.dot(ps[tile(n)][2], vt[g][n - 1], preferred_element_type=F32)
            accs.append(acc)
        out.append([(regroup([ms[tile(n)] for n in blocks], c), regroup([ls[tile(n)] for n in blocks], c),
                     regroup(accs, c)) for c in range(QCHUNKS)])
    return out


def _merge_all(olds, news):
    ms = [jnp.maximum(o[0], n[0]) for o, n in zip(olds, news)]
    a0 = [jnp.exp2(o[0] - m) for o, m in zip(olds, ms)]
    a1 = [jnp.exp2(n[0] - m) for n, m in zip(news, ms)]
    ls = [x * o[1] + y * n[1] for x, y, o, n in zip(a0, a1, olds, news)]
    accs = [x * o[2] + y * n[2] for x, y, o, n in zip(a0, a1, olds, news)]
    return list(zip(ms, ls, accs))


def _attn_kernel(q_ref, k_ref, v_ref, o_ref, acc_ref, m_ref, l_ref):
    ua1, ub1 = _tile_order(1)
    ua4, ub4 = _tile_order(4)
    cur1, cur4, prev4 = ub1 <= ua1, ub4 <= ua4, ub4 >= ua4
    causal = (lax.broadcasted_iota(jnp.int32, (SUB_BLOCK, SUB_BLOCK), 1)
              <= lax.broadcasted_iota(jnp.int32, (SUB_BLOCK, SUB_BLOCK), 0))
    heads = [slice(hh * HEAD_DIM, (hh + 1) * HEAD_DIM) for hh in range(ATT_HEADS)]
    bf = lambda t: t.astype(BF16)

    def first(r4, carry):
        classes = [r4 + QCHUNKS * c for c in range(QCHUNKS)]
        groups = [tuple([bf(ref[r, :, cs]) for r in classes] for ref in (q_ref, k_ref, v_ref)) for cs in heads]
        for hh, per_class in enumerate(_first_pass(groups, cur4, prev4, causal)):
            for r, (m, l, acc) in zip(classes, per_class):
                m_ref[hh, r] = m
                l_ref[hh, r] = l
                acc_ref[hh, r] = acc
        return carry
    lax.fori_loop(0, QCHUNKS, first, 0)

    rows = N_CLASSES * SUBLANES

    def d1(n2, carry):
        start = pl.multiple_of(n2 * 2 * SUBLANES, 2 * SUBLANES)
        prev = pl.multiple_of(jnp.maximum(n2 * 2 - 1, 0) * SUBLANES, SUBLANES)
        prev1 = ub1 >= ua1
        prev1_first = ub1 >= ua1 + jnp.where(n2 > 0, 0, SUB_BLOCK)
        pair = pl.ds(start, 2 * SUBLANES)
        lo, hi = slice(0, SUBLANES), slice(SUBLANES, 2 * SUBLANES)
        flat = lambda t: t.reshape(rows, t.shape[-1])
        tiles, olds = [], []
        for hh, cs in enumerate(heads):
            qb, kb, vb = q_ref[:, pair, cs], k_ref[:, pair, cs], v_ref[:, pair, cs]
            kp, vp = k_ref[:, pl.ds(prev, SUBLANES), cs], v_ref[:, pl.ds(prev, SUBLANES), cs]
            mb, lb, ab = m_ref[hh, :, pair, :], l_ref[hh, :, pair, :], acc_ref[hh, :, pair, :]
            for half, kprev, vprev, pmask in ((lo, kp, vp, prev1_first), (hi, kb[:, lo], vb[:, lo], prev1)):
                tiles.append((bf(flat(qb[:, half])),
                              [bf(flat(kprev)), bf(flat(kb[:, half]))],
                              [bf(flat(vprev)), bf(flat(vb[:, half]))], [pmask, cur1]))
                olds.append((flat(mb[:, half]), flat(lb[:, half]), flat(ab[:, half])))
        outs = [(acc / l).reshape(N_CLASSES, SUBLANES, HEAD_DIM)
                for _, l, acc in _merge_all(olds, _softmax_tiles(tiles))]
        for hh, cs in enumerate(heads):
            o_ref[:, pair, cs] = jnp.concatenate(outs[2 * hh:2 * hh + 2], axis=1).astype(o_ref.dtype)
        return carry
    lax.fori_loop(0, CLASS_LEN // (2 * SUBLANES), d1, 0)


def _attn(qkv):
    blk = (None, N_CLASSES, CLASS_LEN, ATT_HEADS * HEAD_DIM)
    n_hg = N_HEADS // ATT_HEADS
    tile = _nbytes((N_CLASSES, CLASS_LEN, ATT_HEADS * HEAD_DIM), F32)
    state = ATT_HEADS * _nbytes((N_CLASSES, CLASS_LEN, HEAD_DIM), F32)
    vmem = _vmem_limit(6 * tile, tile, 3 * state)
    return pl.pallas_call(
        _attn_kernel,
        grid=(BATCH, n_hg),
        in_specs=[
            pl.BlockSpec(blk, lambda b, g: (b, 0, 0, g)),
            pl.BlockSpec(blk, lambda b, g: (b, 0, 0, n_hg + g)),
            pl.BlockSpec(blk, lambda b, g: (b, 0, 0, 2 * n_hg + g)),
        ],
        out_specs=pl.BlockSpec(blk, lambda b, g: (b, 0, 0, g)),
        out_shape=jax.ShapeDtypeStruct((BATCH, N_CLASSES, CLASS_LEN, D_MODEL), BF16),
        scratch_shapes=[
            pltpu.VMEM((ATT_HEADS, N_CLASSES, CLASS_LEN, HEAD_DIM), F32),
            pltpu.VMEM((ATT_HEADS, N_CLASSES, CLASS_LEN, LANES), F32),
            pltpu.VMEM((ATT_HEADS, N_CLASSES, CLASS_LEN, LANES), F32),
        ],
        compiler_params=pltpu.CompilerParams(
            dimension_semantics=("parallel", "parallel"), vmem_limit_bytes=vmem),
        name="attn_core",
    )(qkv, qkv, qkv)


def _attn_out_kernel(a_ref, p_ref, w_ref, x_ref, o_ref, l_ref):
    @pl.when(pl.program_id(2) == 0)
    def _():
        for g in range(ATT_TM // PERM_GROUP):
            ag = a_ref[:, g * N_CLASSES:(g + 1) * N_CLASSES, :].reshape(PERM_GROUP, D_MODEL)
            l_ref[g * PERM_GROUP:(g + 1) * PERM_GROUP, :] = jnp.dot(
                p_ref[...], ag, preferred_element_type=F32).astype(BF16)

    o_ref[...] = x_ref[...] + jnp.dot(l_ref[...], w_ref[...], preferred_element_type=F32)


def _attn_out(att, w_o, x, perm, layer):
    xblk = pl.BlockSpec((None, ATT_TM, AO_TN), lambda b, m, n: (b, m, n))
    vmem = _vmem_limit(
        3 * _nbytes((ATT_TM, D_MODEL), BF16), 2 * _nbytes((D_MODEL, AO_TN), BF16),
        4 * _nbytes((ATT_TM, AO_TN), F32))
    return pl.pallas_call(
        _attn_out_kernel,
        grid=(BATCH, SEQ // ATT_TM, D_MODEL // AO_TN),
        in_specs=[
            pl.BlockSpec((None, N_CLASSES, ATT_JB, D_MODEL), lambda b, m, n: (b, 0, m, 0)),
            pl.BlockSpec(perm.shape, lambda b, m, n: (0, 0)),
            pl.BlockSpec((None, D_MODEL, AO_TN), lambda b, m, n: (layer, 0, n)),
            xblk,
        ],
        out_specs=xblk,
        out_shape=jax.ShapeDtypeStruct(x.shape, F32),
        scratch_shapes=[pltpu.VMEM((ATT_TM, D_MODEL), BF16)],
        compiler_params=pltpu.CompilerParams(
            dimension_semantics=("parallel", "parallel", "arbitrary"), vmem_limit_bytes=vmem),
        name="attn_out",
    )(att, perm, w_o, x)


def kernel(x, mix_norm, mlp_norm, final_norm, mlp_w1, mlp_w2,
           lru_w_in, lru_conv_w, lru_conv_b, lru_w_a, lru_b_a, lru_w_x, lru_b_x,
           lru_lambda, lru_w_out, attn_w_qkv, attn_w_o):
    assert x.shape == (BATCH, SEQ, D_MODEL) and x.dtype == F32
    row = lambda v: v.reshape(1, -1)
    tables = _rope_tables()
    perm_bt = _swap_perm(BATCH, SUBLANES)
    perm_sq = _swap_perm(N_CLASSES, N_CLASSES)
    gf = row(final_norm)
    (mlp_w1, mlp_w2, lru_w_in, lru_w_a, lru_w_x, lru_w_out, attn_w_qkv, attn_w_o) = (
        w.astype(BF16) for w in (mlp_w1, mlp_w2, lru_w_in, lru_w_a, lru_w_x, lru_w_out, attn_w_qkv, attn_w_o))
    for i in range(DEPTH):
        j = i // N_MIXERS
        g = row(mix_norm[i])
        if i % N_MIXERS == 0:
            xg = _lru_in(x, g, lru_w_in, perm_bt, j)
            x = _lru_tail(xg, lru_conv_w[j], row(lru_conv_b[j]), lru_w_a, row(lru_b_a[j]),
                          lru_w_x, row(lru_b_x[j]), row(lru_lambda[j]), lru_w_out, x, perm_sq, j)
        else:
            qkv = _qkv(x, g, attn_w_qkv, tables, perm_sq, j)
            att = _attn(qkv)
            x = _attn_out(att, attn_w_o, x, perm_sq, j)
        x2d = _mlp(x.reshape(BATCH * SEQ, D_MODEL), row(mlp_norm[i]), mlp_w1, mlp_w2, gf, i == DEPTH - 1, i)
        x = x2d.reshape(BATCH, SEQ, D_MODEL)
    return x
```
